```python
import math, functools
import jax, jax.numpy as jnp
from jax import lax
import numpy as np

D_MODEL = 1024
BATCH = 8
SEQ = 2048
DEPTH = 2
DEC_BATCH = 128
DEC_SEQ = 4
PAST_LEN = 8192
PAGE_SIZE = 128

A_HEADS = 4
A_KV_HEADS = 2
A_QK_DIM = 64
A_V_DIM = 2 * A_QK_DIM
B_HEADS = 4
B_HEAD_DIM = 64
MOBA_BLOCK = 256
MOBA_TOPK = 3
MOBA_Q_CHUNK = 64
C_HEADS = 4
C_Q_RANK = 256
C_KV_RANK = 128
C_NOPE_DIM = 64
C_ROPE_DIM = 32
C_V_DIM = 64
C_SCALE = (C_NOPE_DIM + C_ROPE_DIM) ** -0.5
ROPE_BASE = 10000.0
D_FF = 2816
Q_BLOCK = 128
NORM_EPS = 1e-6

SPLIT_SIZES = (A_HEADS * 2 * A_QK_DIM, A_KV_HEADS * 2 * A_QK_DIM, A_KV_HEADS * A_V_DIM,
               B_HEADS * B_HEAD_DIM, B_HEADS * B_HEAD_DIM, B_HEADS * B_HEAD_DIM,
               C_Q_RANK, C_KV_RANK, C_ROPE_DIM)
IN_COLS = sum(SPLIT_SIZES)
MIX_WIDTH = A_HEADS * A_V_DIM + B_HEADS * B_HEAD_DIM + C_HEADS * C_V_DIM

kernel_name = 'hybrid_diff_moba_mla_step'


def rms_norm(x, g):
    xf = x.astype(jnp.float32)
    y = xf * lax.rsqrt(jnp.mean(xf * xf, axis=-1, keepdims=True) + NORM_EPS)
    return (y * g.astype(jnp.float32)).astype(x.dtype)


def swiglu(x, w_gate_up, w_down):
    g, u = jnp.split(x @ w_gate_up, 2, axis=-1)
    return (jax.nn.silu(g) * u) @ w_down


def alibi_slopes():
    n = A_HEADS + B_HEADS
    s = 2.0 ** (-8.0 * jnp.arange(1, n + 1, dtype=jnp.float32) / n)
    return s[0::2], s[1::2]


def rope(x, pos):
    half = C_ROPE_DIM // 2
    inv = ROPE_BASE ** (-jnp.arange(half, dtype=jnp.float32) / half)
    ang = pos.astype(jnp.float32)[:, None] * inv[None, :]
    shape = (1, pos.shape[0]) + (1,) * (x.ndim - 3) + (half,)
    cos = jnp.cos(ang).reshape(shape)
    sin = jnp.sin(ang).reshape(shape)
    x1 = x[..., :half].astype(jnp.float32)
    x2 = x[..., half:].astype(jnp.float32)
    return jnp.concatenate([x1 * cos - x2 * sin, x2 * cos + x1 * sin], axis=-1).astype(x.dtype)


def diff_lambda_value(lp, lam_init):
    f = lp.astype(jnp.float32)
    return jnp.exp(jnp.sum(f[0] * f[1])) - jnp.exp(jnp.sum(f[2] * f[3])) + lam_init


def gather_pages(pool, page_table):
    g = pool[page_table]
    return g.reshape((page_table.shape[0], -1) + pool.shape[2:])


def sweep_query_blocks(fn, arrays, q_pos, block):
    s = q_pos.shape[0]
    nb = s // block
    def blk(a):
        return jnp.swapaxes(a.reshape((a.shape[0], nb, block) + a.shape[2:]), 0, 1)
    xs = tuple(blk(a) for a in arrays) + (q_pos.reshape(nb, block),)
    out = lax.map(lambda t: fn(*t), xs)
    out = jnp.swapaxes(out, 0, 1)
    return out.reshape((out.shape[0], s) + out.shape[3:])


def diff_attn_core(q, k, v, q_pos, k_pos, slopes, lam):
    n, tq = q.shape[:2]
    rep = A_HEADS // A_KV_HEADS
    qg = q.reshape(n, tq, A_KV_HEADS, rep, 2, A_QK_DIM)
    s = jnp.einsum('nqgrmd,nkgmd->ngrmqk', qg, k).astype(jnp.float32) * (A_QK_DIM ** -0.5)
    dist = q_pos[:, None] - k_pos[None, :]
    bias = slopes.reshape(A_KV_HEADS, rep)[None, :, :, None, None, None] * dist.astype(jnp.float32)
    s = jnp.where(dist >= 0, s - bias, -jnp.inf)
    p = jax.nn.softmax(s, axis=-1)
    pd = p[:, :, :, 0] - lam * p[:, :, :, 1]
    o = jnp.einsum('ngrqk,nkgv->nqgrv', pd.astype(v.dtype), v)
    return o.reshape(n, tq, A_HEADS, A_V_DIM)


def mla_core(q_lat, q_rope, lat, krope, q_pos, k_pos):
    s = (jnp.einsum('nqhc,nkc->nhqk', q_lat, lat)
         + jnp.einsum('nqhr,nkr->nhqk', q_rope, krope)).astype(jnp.float32) * C_SCALE
    dist = q_pos[:, None] - k_pos[None, :]
    s = jnp.where(dist >= 0, s, -jnp.inf)
    p = jax.nn.softmax(s, axis=-1)
    return jnp.einsum('nhqk,nkc->nqhc', p.astype(lat.dtype), lat)


def block_view(k):
    n, l, h, d = k.shape
    return k.reshape(n, l // MOBA_BLOCK, MOBA_BLOCK, h, d).transpose(0, 3, 1, 2, 4)


def gather_blocks(kb, idx):
    n, h = kb.shape[:2]
    n_ar = jnp.arange(n)[:, None, None, None]
    h_ar = jnp.arange(h)[None, None, :, None]
    g = kb[n_ar, h_ar, idx]
    return g.reshape(g.shape[:3] + (-1, g.shape[-1]))


def sel_positions(idx):
    pos = idx[..., None] * MOBA_BLOCK + jnp.arange(MOBA_BLOCK, dtype=idx.dtype)
    return pos.reshape(idx.shape[:3] + (-1,))


def moba_core(q, q_pos, sel, k_own, v_own, own_pos, slopes):
    scale = B_HEAD_DIM ** -0.5
    sl = slopes[None, None, :, None]
    d_own = q_pos[:, None] - own_pos[None, :]
    blk0 = (q_pos // MOBA_BLOCK) * MOBA_BLOCK
    ok_own = (d_own >= 0) & (own_pos[None, :] >= blk0[:, None])
    s_own = (jnp.einsum('nthd,nlhd->nthl', q, k_own).astype(jnp.float32) * scale
             - sl * d_own[None, :, None, :].astype(jnp.float32))
    s_own = jnp.where(ok_own[None, :, None, :], s_own, -jnp.inf)
    if sel is None:
        p = jax.nn.softmax(s_own, axis=-1)
        return jnp.einsum('nthl,nlhd->nthd', p.astype(v_own.dtype), v_own)
    k_sel, v_sel, sel_pos, sel_ok = sel
    d_sel = (q_pos[None, :, None, None] - sel_pos).astype(jnp.float32)
    s_sel = jnp.einsum('nthd,nthjd->nthj', q, k_sel).astype(jnp.float32) * scale - sl * d_sel
    s_sel = jnp.where(sel_ok, s_sel, -jnp.inf)
    p = jax.nn.softmax(jnp.concatenate([s_sel, s_own], axis=-1), axis=-1)
    j = s_sel.shape[-1]
    return (jnp.einsum('nthj,nthjd->nthd', p[..., :j].astype(v_sel.dtype), v_sel)
            + jnp.einsum('nthl,nlhd->nthd', p[..., j:].astype(v_own.dtype), v_own))


def moba_prompt(q, k, v, q_pos, slopes):
    n, s = q.shape[:2]
    nb = -(-s // MOBA_BLOCK)
    pad = nb * MOBA_BLOCK - s
    k_pad = jnp.pad(k, ((0, 0), (0, pad), (0, 0), (0, 0)))
    v_pad = jnp.pad(v, ((0, 0), (0, pad), (0, 0), (0, 0)))
    kk = min(MOBA_TOPK, nb - 1)

    def own_block(pos_c):
        start = (pos_c[0] // MOBA_BLOCK) * MOBA_BLOCK
        k_own = lax.dynamic_slice_in_dim(k_pad, start, MOBA_BLOCK, axis=1)
        v_own = lax.dynamic_slice_in_dim(v_pad, start, MOBA_BLOCK, axis=1)
        return k_own, v_own, start + jnp.arange(MOBA_BLOCK, dtype=jnp.int32)

    if kk == 0:
        def chunk0(q_c, pos_c):
            k_own, v_own, own_pos = own_block(pos_c)
            return moba_core(q_c, pos_c, None, k_own, v_own, own_pos, slopes)
        return sweep_query_blocks(chunk0, (q,), q_pos, MOBA_Q_CHUNK)

    kb = block_view(k_pad)
    vb = block_view(v_pad)
    k_mean = jnp.mean(kb.astype(jnp.float32), axis=3)
    gate = jnp.einsum('nthd,nhbd->nthb', q.astype(jnp.float32), k_mean)
    q_blk = q_pos // MOBA_BLOCK
    past_ok = jnp.arange(nb)[None, :] < q_blk[:, None]
    gate = jnp.where(past_ok[None, :, None, :], gate, -jnp.inf)
    _, idx = lax.top_k(gate, kk)
    ok = idx < q_blk[None, :, None, None]

    def chunk(q_c, idx_c, ok_c, pos_c):
        k_own, v_own, own_pos = own_block(pos_c)
        sel = (gather_blocks(kb, idx_c), gather_blocks(vb, idx_c), sel_positions(idx_c),
               jnp.repeat(ok_c, MOBA_BLOCK, axis=-1))
        return moba_core(q_c, pos_c, sel, k_own, v_own, own_pos, slopes)

    return sweep_query_blocks(chunk, (q, idx, ok), q_pos, MOBA_Q_CHUNK)


def moba_sample(q, k_new, v_new, q_pos, pool_k, pool_v, page_table, slopes):
    n, t = q.shape[:2]
    page = pool_k.shape[1]
    past_len = page_table.shape[1] * page
    past_k = gather_pages(pool_k, page_table)
    n_full = past_len // MOBA_BLOCK
    s0 = n_full * MOBA_BLOCK
    kk = min(MOBA_TOPK, n_full)
    past_v_own = gather_pages(pool_v, page_table[:, s0 // page:])
    k_own = jnp.concatenate([past_k[:, s0:], k_new], axis=1)
    v_own = jnp.concatenate([past_v_own, v_new], axis=1)
    own_pos = s0 + jnp.arange(past_len - s0 + t, dtype=jnp.int32)
    if kk == 0:
        return moba_core(q, q_pos, None, k_own, v_own, own_pos, slopes)
    kb = block_view(past_k[:, :s0])
    k_mean = jnp.mean(kb.astype(jnp.float32), axis=3)
    gate = jnp.einsum('nthd,nhbd->nthb', q.astype(jnp.float32), k_mean)
    _, idx = lax.top_k(gate, kk)
    k_sel = gather_blocks(kb, idx)
    ppb = MOBA_BLOCK // page
    n_ar = jnp.arange(n)[:, None, None, None, None]
    pages = page_table[n_ar, idx[..., None] * ppb + jnp.arange(ppb, dtype=idx.dtype)]
    h_ar = jnp.arange(B_HEADS)[None, None, :, None, None]
    v_sel = pool_v[pages, :, h_ar]
    v_sel = v_sel.reshape(v_sel.shape[:3] + (-1, B_HEAD_DIM))
    sel_pos = sel_positions(idx)
    sel = (k_sel, v_sel, sel_pos, jnp.ones(sel_pos.shape, dtype=bool))
    return moba_core(q, q_pos, sel, k_own, v_own, own_pos, slopes)


def project(u, pos, w_in_l, q_norm, w_uq, kv_norm, w_ukv):
    n, t, _ = u.shape
    offsets = [int(o) for o in np.cumsum(SPLIT_SIZES)[:-1]]
    a_q, a_k, a_v, b_q, b_k, b_v, c_q, c_kv, c_kr = jnp.split(u @ w_in_l, offsets, axis=-1)
    a_q = a_q.reshape(n, t, A_HEADS, 2, A_QK_DIM)
    a_k = a_k.reshape(n, t, A_KV_HEADS, 2, A_QK_DIM)
    a_v = a_v.reshape(n, t, A_KV_HEADS, A_V_DIM)
    b_q = b_q.reshape(n, t, B_HEADS, B_HEAD_DIM)
    b_k = b_k.reshape(n, t, B_HEADS, B_HEAD_DIM)
    b_v = b_v.reshape(n, t, B_HEADS, B_HEAD_DIM)
    q = (rms_norm(c_q, q_norm) @ w_uq).reshape(n, t, C_HEADS, C_NOPE_DIM + C_ROPE_DIM)
    q_rope = rope(q[..., C_NOPE_DIM:], pos)
    w_uk = w_ukv.reshape(C_KV_RANK, C_HEADS, C_NOPE_DIM + C_V_DIM)[..., :C_NOPE_DIM]
    q_lat = jnp.einsum('nthd,chd->nthc', q[..., :C_NOPE_DIM], w_uk)
    c_lat = rms_norm(c_kv, kv_norm)
    c_kr = rope(c_kr, pos)
    return (a_q, a_k, a_v, b_q, b_k, b_v, q_lat, q_rope, c_lat, c_kr)


def prompt_attend(pr, q_pos, lam, slopes_a, slopes_b):
    a_q, a_k, a_v, b_q, b_k, b_v, q_lat, q_rope, c_lat, c_kr = pr
    oa = sweep_query_blocks(lambda qb, pb: diff_attn_core(qb, a_k, a_v, pb, q_pos, slopes_a, lam),
                            (a_q,), q_pos, Q_BLOCK)
    ob = moba_prompt(b_q, b_k, b_v, q_pos, slopes_b)
    oc = sweep_query_blocks(lambda ql, qr, pb: mla_core(ql, qr, c_lat, c_kr, pb, q_pos),
                            (q_lat, q_rope), q_pos, Q_BLOCK)
    return oa, ob, oc


def sample_attend(pr, q_pos, ca_k, ca_v, cb_k, cb_v, cc_lat, cc_kr, page_table, lam, slopes_a, slopes_b):
    a_q, a_k, a_v, b_q, b_k, b_v, q_lat, q_rope, c_lat, c_kr = pr
    n = a_q.shape[0]
    past_len = page_table.shape[1] * ca_k.shape[1]
    k_pos = jnp.concatenate([jnp.arange(past_len, dtype=jnp.int32), q_pos])
    ka = jnp.concatenate([gather_pages(ca_k, page_table).reshape(n, past_len, A_KV_HEADS, 2, A_QK_DIM), a_k], axis=1)
    va = jnp.concatenate([gather_pages(ca_v, page_table), a_v], axis=1)
    oa = diff_attn_core(a_q, ka, va, q_pos, k_pos, slopes_a, lam)
    ob = moba_sample(b_q, b_k, b_v, q_pos, cb_k, cb_v, page_table, slopes_b)
    lat = jnp.concatenate([gather_pages(cc_lat, page_table), c_lat], axis=1)
    kr = jnp.concatenate([gather_pages(cc_kr, page_table), c_kr], axis=1)
    oc = mla_core(q_lat, q_rope, lat, kr, q_pos, k_pos)
    return oa, ob, oc


def merge_heads(oa, ob, oc_lat, lam_init, subln, w_ukv, w_out_l):
    n, t = oa.shape[:2]
    oa = rms_norm(oa, subln) * (1.0 - lam_init)
    w_uv = w_ukv.reshape(C_KV_RANK, C_HEADS, C_NOPE_DIM + C_V_DIM)[..., C_NOPE_DIM:]
    oc = jnp.einsum('nthc,chv->nthv', oc_lat, w_uv)
    o = jnp.concatenate([oa.reshape(n, t, -1), ob.reshape(n, t, -1), oc.reshape(n, t, -1)], axis=-1)
    return o @ w_out_l


def trunk_layer(x, pos, attend, lam_init, gains, w_gu, w_dn, w_in_l, w_out_l, subln, q_norm, w_uq, kv_norm, w_ukv):
    h = x + 0.5 * rms_norm(swiglu(rms_norm(x, gains[0]), w_gu[0], w_dn[0]), gains[1])
    pr = project(rms_norm(h, gains[2]), pos, w_in_l, q_norm, w_uq, kv_norm, w_ukv)
    oa, ob, oc = attend(pr)
    h = h + rms_norm(merge_heads(oa, ob, oc, lam_init, subln, w_ukv, w_out_l), gains[3])
    h = h + 0.5 * rms_norm(swiglu(rms_norm(h, gains[4]), w_gu[1], w_dn[1]), gains[5])
    n, t = x.shape[:2]
    rows = (pr[1].reshape(n, t, A_KV_HEADS, 2 * A_QK_DIM), pr[2], pr[4], pr[5], pr[8], pr[9])
    return h, rows


def stack_layers(rows, i):
    return jnp.stack([r[i] for r in rows])


def setup_inputs(seed: int = 0) -> dict:
    key = jax.random.key(seed)
    ks = jax.random.split(key, 24)
    f32 = jnp.float32
    n_pages = PAST_LEN // PAGE_SIZE
    n_pool = (DEC_BATCH * n_pages * 5) // 4

    def nrm(k, shape, scale):
        return jax.random.normal(k, shape, f32) * scale

    def gain(k, shape):
        return 1.0 + 0.02 * jax.random.normal(k, shape, f32)

    page_table = jax.random.permutation(ks[8], n_pool)[:DEC_BATCH * n_pages]
    page_table = page_table.reshape(DEC_BATCH, n_pages).astype(jnp.int32)
    return {
        'x_prompt': nrm(ks[0], (BATCH, SEQ, D_MODEL), 1.0),
        'x_sample': nrm(ks[1], (DEC_BATCH, DEC_SEQ, D_MODEL), 1.0),
        'cache_a_k': nrm(ks[2], (DEPTH, n_pool, PAGE_SIZE, A_KV_HEADS, 2 * A_QK_DIM), 1.0),
        'cache_a_v': nrm(ks[3], (DEPTH, n_pool, PAGE_SIZE, A_KV_HEADS, A_V_DIM), 1.0),
        'cache_b_k': nrm(ks[4], (DEPTH, n_pool, PAGE_SIZE, B_HEADS, B_HEAD_DIM), 1.0),
        'cache_b_v': nrm(ks[5], (DEPTH, n_pool, PAGE_SIZE, B_HEADS, B_HEAD_DIM), 1.0),
        'cache_c_latent': nrm(ks[6], (DEPTH, n_pool, PAGE_SIZE, C_KV_RANK), 1.0),
        'cache_c_krope': nrm(ks[7], (DEPTH, n_pool, PAGE_SIZE, C_ROPE_DIM), 1.0),
        'page_table': page_table,
        'norm_gains': gain(ks[9], (DEPTH, 6, D_MODEL)),
        'ffn_w_gate_up': nrm(ks[10], (DEPTH, 2, D_MODEL, 2 * D_FF), D_MODEL ** -0.5),
        'ffn_w_down': nrm(ks[11], (DEPTH, 2, D_FF, D_MODEL), D_FF ** -0.5),
        'w_in': nrm(ks[12], (DEPTH, D_MODEL, IN_COLS), D_MODEL ** -0.5),
        'w_out': nrm(ks[13], (DEPTH, MIX_WIDTH, D_MODEL), MIX_WIDTH ** -0.5),
        'diff_lambda': nrm(ks[14], (DEPTH, 4, A_QK_DIM), 0.1),
        'diff_subln': gain(ks[15], (DEPTH, A_V_DIM)),
        'mla_q_norm': gain(ks[16], (DEPTH, C_Q_RANK)),
        'mla_w_uq': nrm(ks[17], (DEPTH, C_Q_RANK, C_HEADS * (C_NOPE_DIM + C_ROPE_DIM)), C_Q_RANK ** -0.5),
        'mla_kv_norm': gain(ks[18], (DEPTH, C_KV_RANK)),
        'mla_w_ukv': nrm(ks[19], (DEPTH, C_KV_RANK, C_HEADS * (C_NOPE_DIM + C_V_DIM)), C_KV_RANK ** -0.5),
    }


def reference(x_prompt, x_sample, cache_a_k, cache_a_v, cache_b_k, cache_b_v, cache_c_latent, cache_c_krope,
              page_table, norm_gains, ffn_w_gate_up, ffn_w_down, w_in, w_out, diff_lambda, diff_subln,
              mla_q_norm, mla_w_uq, mla_kv_norm, mla_w_ukv):
    slopes_a, slopes_b = alibi_slopes()
    past_len = page_table.shape[1] * cache_a_k.shape[2]
    pos_p = jnp.arange(x_prompt.shape[1], dtype=jnp.int32)
    pos_s = past_len + jnp.arange(x_sample.shape[1], dtype=jnp.int32)
    h_p, h_s = x_prompt, x_sample
    rows_p, rows_s = [], []
    for l in range(DEPTH):
        lam_init = 0.8 - 0.6 * math.exp(-0.3 * l)
        lam = diff_lambda_value(diff_lambda[l], lam_init)
        weights = (norm_gains[l], ffn_w_gate_up[l], ffn_w_down[l], w_in[l], w_out[l], diff_subln[l],
                   mla_q_norm[l], mla_w_uq[l], mla_kv_norm[l], mla_w_ukv[l])
        attend_p = functools.partial(prompt_attend, q_pos=pos_p, lam=lam, slopes_a=slopes_a, slopes_b=slopes_b)
        attend_s = functools.partial(sample_attend, q_pos=pos_s, ca_k=cache_a_k[l], ca_v=cache_a_v[l],
                                     cb_k=cache_b_k[l], cb_v=cache_b_v[l], cc_lat=cache_c_latent[l],
                                     cc_kr=cache_c_krope[l], page_table=page_table, lam=lam,
                                     slopes_a=slopes_a, slopes_b=slopes_b)
        h_p, r_p = trunk_layer(h_p, pos_p, attend_p, lam_init, *weights)
        h_s, r_s = trunk_layer(h_s, pos_s, attend_s, lam_init, *weights)
        rows_p.append(r_p)
        rows_s.append(r_s)
    return (h_p, h_s,
            stack_layers(rows_p, 0), stack_layers(rows_p, 1), stack_layers(rows_p, 2),
            stack_layers(rows_p, 3), stack_layers(rows_p, 4), stack_layers(rows_p, 5),
            stack_layers(rows_s, 0), stack_layers(rows_s, 1), stack_layers(rows_s, 2),
            stack_layers(rows_s, 3), stack_layers(rows_s, 4), stack_layers(rows_s, 5))
```

```python
import functools
import math

import numpy as np
import jax
import jax.numpy as jnp
from jax import lax
from jax.experimental import pallas as pl
from jax.experimental.pallas import tpu as pltpu

F32 = jnp.float32
BF16 = jnp.bfloat16

D_MODEL = 1024
D_FF = 2816
A_HEADS = 4
A_KV_HEADS = 2
A_QK_DIM = 64
A_V_DIM = 128
B_HEADS = 4
B_HEAD_DIM = 64
MOBA_BLOCK = 256
MOBA_TOPK = 3
C_HEADS = 4
C_Q_RANK = 256
C_KV_RANK = 128
C_NOPE_DIM = 64
C_ROPE_DIM = 32
C_V_DIM = 64
C_SCALE = (C_NOPE_DIM + C_ROPE_DIM) ** -0.5
ROPE_BASE = 10000.0
NORM_EPS = 1e-6
QK_SCALE = 0.125

IN_COLS = 2208
IN_PAD = 2304
TQ = 256
PAGE = 128
NEG_INF = float("-inf")

VMEM_LIMIT = 56 * 1024 * 1024

SLOPES_A = tuple(2.0 ** -(2 * h + 1) for h in range(A_HEADS))
SLOPES_B = tuple(2.0 ** -(2 * h + 2) for h in range(B_HEADS))


def _rms(x, g):
    return x * lax.rsqrt(jnp.mean(x * x, axis=-1, keepdims=True) + NORM_EPS) * g


def _dot(a, b):
    return jnp.dot(a, b, preferred_element_type=F32)


def _dot_nt(a, b, precision=None):
    return lax.dot_general(a, b, (((1,), (1,)), ((), ())), preferred_element_type=F32,
                           precision=precision)


def _const_spec(shape):
    nd = len(shape)
    return pl.BlockSpec(shape, lambda *_: (0,) * nd, pipeline_mode=pl.Buffered(1))


def _params(sem):
    return pltpu.CompilerParams(dimension_semantics=sem, vmem_limit_bytes=VMEM_LIMIT)


FFN_CHUNKS = 2
FFN_CHUNK = D_FF // FFN_CHUNKS


def _ffn_body(x, gpre, gpost, wgu_ref, wdn_ref):
    xn = _rms(x, gpre).astype(BF16)
    acc = None
    for c in range(FFN_CHUNKS):
        lo = c * FFN_CHUNK
        g = _dot(xn, wgu_ref[:, lo:lo + FFN_CHUNK])
        u = _dot(xn, wgu_ref[:, D_FF + lo:D_FF + lo + FFN_CHUNK])
        a = (g * (1.0 / (1.0 + jnp.exp(-g))) * u).astype(BF16)
        part = _dot(a, wdn_ref[lo:lo + FFN_CHUNK, :])
        acc = part if acc is None else acc + part
    return x + 0.5 * _rms(acc, gpost)


def _ffn_kernel(x_ref, gpre_ref, gpost_ref, wgu_ref, wdn_ref, o_ref):
    o_ref[...] = _ffn_body(x_ref[...], gpre_ref[...], gpost_ref[...], wgu_ref, wdn_ref)


def _ffn(x, gpre, gpost, wgu, wdn, tm):
    t = x.shape[0]
    return pl.pallas_call(
        _ffn_kernel,
        out_shape=jax.ShapeDtypeStruct((t, D_MODEL), F32),
        grid=(t // tm,),
        in_specs=[pl.BlockSpec((tm, D_MODEL), lambda i: (i, 0)),
                  _const_spec((1, D_MODEL)), _const_spec((1, D_MODEL)),
                  _const_spec((D_MODEL, 2 * D_FF)), _const_spec((D_FF, D_MODEL))],
        out_specs=pl.BlockSpec((tm, D_MODEL), lambda i: (i, 0)),
        compiler_params=_params(("parallel",)),
        name="ffn",
    )(x, gpre, gpost, wgu, wdn)


def _proj_kernel(h_ref, g_ref, win_ref, qn_ref, wuq_ref, wqc_ref, kvn_ref,
                 qcs_ref, qsn_ref, kcs_ref, ksn_ref,
                 aq_ref, ak_ref, av_ref, bq_ref, bqf_ref, bk_ref, bv_ref, qc_ref, clat_ref, ckr_ref,
                 akb_ref, avb_ref, bkb_ref, bvb_ref, ckvb_ref):
    u = _rms(h_ref[...], g_ref[...]).astype(BF16)
    z = _dot(u, win_ref[...])
    aq_ref[...] = (z[:, 0:512] * QK_SCALE).astype(BF16)
    ak = z[:, 512:768]
    ak_ref[...] = ak
    akb_ref[...] = ak.astype(BF16)
    av = z[:, 768:1024]
    av_ref[...] = av
    avb_ref[...] = av.astype(BF16)
    bq = z[:, 1024:1280]
    bqf_ref[...] = bq
    bq_ref[...] = (bq * QK_SCALE).astype(BF16)
    bk = z[:, 1280:1536]
    bk_ref[...] = bk
    bkb_ref[...] = bk.astype(BF16)
    bv = z[:, 1536:1792]
    bv_ref[...] = bv
    bvb_ref[...] = bv.astype(BF16)
    cqn = _rms(z[:, 1792:2048], qn_ref[...]).astype(BF16)
    q = _dot(cqn, wuq_ref[...])
    qr = q[:, 256:384]
    qrot = qr * qcs_ref[...] + pltpu.roll(qr, 64, 1) * qsn_ref[...]
    qcat = jnp.concatenate([q[:, 0:256], qrot], axis=1).astype(BF16)
    qc_ref[...] = _dot(qcat, wqc_ref[...]).astype(BF16)
    clat = _rms(z[:, 2048:2176], kvn_ref[...])
    clat_ref[...] = clat
    kr = z[:, 2176:2304]
    lane = lax.broadcasted_iota(jnp.int32, kr.shape, 1)
    swapped = jnp.where(lane < 16, pltpu.roll(kr, 112, 1), pltpu.roll(kr, 16, 1))
    krot = kr * kcs_ref[...] + swapped * ksn_ref[...]
    ckr_ref[...] = krot[:, 0:C_ROPE_DIM]
    ckvb_ref[...] = jnp.concatenate([clat, krot], axis=1).astype(BF16)


def _proj(h, g, win, qn, wuq, wqc, kvn, tabs, tm, tab_blocks):
    t = h.shape[0]
    row = lambda w: pl.BlockSpec((tm, w), lambda i: (i, 0))
    tab = pl.BlockSpec((tm, 128), lambda i: (i % tab_blocks, 0))
    shapes = [(512, BF16), (256, F32), (256, F32), (256, BF16), (256, F32), (256, F32), (256, F32),
              (1024, BF16), (128, F32), (C_ROPE_DIM, F32),
              (256, BF16), (256, BF16), (256, BF16), (256, BF16), (256, BF16)]
    return pl.pallas_call(
        _proj_kernel,
        out_shape=[jax.ShapeDtypeStruct((t, w), d) for w, d in shapes],
        grid=(t // tm,),
        in_specs=[row(D_MODEL), _const_spec((1, D_MODEL)), _const_spec((D_MODEL, IN_PAD)),
                  _const_spec((1, C_Q_RANK)), _const_spec((C_Q_RANK, 384)), _const_spec((384, 1024)),
                  _const_spec((1, C_KV_RANK)), tab, tab, tab, tab],
        out_specs=[row(w) for w, _ in shapes],
        compiler_params=_params(("parallel",)),
        name="proj",
    )(h, g, win, qn, wuq, wqc, kvn, *tabs)


def _lambda_value(lp, lam_init):
    return (jnp.exp(jnp.sum(lp[0:1, :] * lp[1:2, :], axis=-1, keepdims=True))
            - jnp.exp(jnp.sum(lp[2:3, :] * lp[3:4, :], axis=-1, keepdims=True)) + lam_init)


def _causal_tile(reps):
    row = lax.broadcasted_iota(jnp.int32, (TQ, TQ), 0)
    col = lax.broadcasted_iota(jnp.int32, (TQ, TQ), 1)
    m = col <= row
    return jnp.concatenate([m] * reps, axis=0)


def _softmax_parts(s_p, s_d):
    m = jnp.max(s_d, axis=-1, keepdims=True)
    if s_p is not None:
        m = jnp.maximum(m, jnp.max(s_p, axis=-1, keepdims=True))
    p_d = jnp.exp(s_d - m)
    l = jnp.sum(p_d, axis=-1, keepdims=True)
    p_p = None
    if s_p is not None:
        p_p = jnp.exp(s_p - m)
        l = l + jnp.sum(p_p, axis=-1, keepdims=True)
    return p_p, p_d, l


def _attn_a_kernel(q_ref, k_ref, v_ref, lp_ref, subln_ref, o_ref, *, lam_init, seq):
    g = pl.program_id(1)
    lam = _lambda_value(lp_ref[...], lam_init)
    lane = lax.broadcasted_iota(jnp.int32, (TQ, 128), 1)
    is_m0 = lane < A_QK_DIM
    causal = _causal_tile(4)
    rowblk = lax.broadcasted_iota(jnp.int32, (4 * TQ, 1), 0) // TQ
    s_r0 = jnp.where(g == 0, SLOPES_A[0], SLOPES_A[2])
    s_r1 = jnp.where(g == 0, SLOPES_A[1], SLOPES_A[3])
    slope = jnp.where(rowblk % 2 == 0, s_r0, s_r1).astype(F32)
    coef_sign = jnp.where(rowblk < 2, 1.0, -lam)
    rel_d = lax.broadcasted_iota(jnp.int32, (1, TQ), 1).astype(F32)
    for qi in range(seq // TQ):
        r0 = qi * TQ
        q0 = q_ref[r0:r0 + TQ, 0:128]
        q1 = q_ref[r0:r0 + TQ, 128:256]
        zero = jnp.zeros_like(q0)
        qz = jnp.concatenate([jnp.where(is_m0, q0, zero), jnp.where(is_m0, q1, zero),
                              jnp.where(is_m0, zero, q0), jnp.where(is_m0, zero, q1)], axis=0)
        s_d = _dot_nt(qz, k_ref[r0:r0 + TQ, :]) + slope * rel_d
        s_d = jnp.where(causal, s_d, NEG_INF)
        s_p = None
        if qi > 0:
            rel_p = (lax.broadcasted_iota(jnp.int32, (1, r0), 1) - r0).astype(F32)
            s_p = _dot_nt(qz, k_ref[0:r0, :]) + slope * rel_p
        p_p, p_d, l = _softmax_parts(s_p, s_d)
        coef = coef_sign / l
        w_d = p_d * coef
        pd_d = (w_d[0:2 * TQ] + w_d[2 * TQ:4 * TQ]).astype(BF16)
        o = _dot(pd_d, v_ref[r0:r0 + TQ, :])
        if qi > 0:
            w_p = p_p * coef
            pd_p = (w_p[0:2 * TQ] + w_p[2 * TQ:4 * TQ]).astype(BF16)
            o = o + _dot(pd_p, v_ref[0:r0, :])
        y = _rms(o, subln_ref[...]) * (1.0 - lam_init)
        o_ref[r0:r0 + TQ, 0:128] = y[0:TQ].astype(BF16)
        o_ref[r0:r0 + TQ, 128:256] = y[TQ:2 * TQ].astype(BF16)


def _attn_a(aq, akb, avb, lp, subln, lam_init, nb, seq):
    return pl.pallas_call(
        functools.partial(_attn_a_kernel, lam_init=lam_init, seq=seq),
        out_shape=jax.ShapeDtypeStruct((nb * seq, 512), BF16),
        grid=(nb, A_KV_HEADS),
        in_specs=[pl.BlockSpec((seq, 256), lambda n, g: (n, g)),
                  pl.BlockSpec((seq, 128), lambda n, g: (n, g)),
                  pl.BlockSpec((seq, 128), lambda n, g: (n, g)),
                  _const_spec((4, A_QK_DIM)), _const_spec((1, A_V_DIM))],
        out_specs=pl.BlockSpec((seq, 256), lambda n, g: (n, g)),
        compiler_params=_params(("parallel", "parallel")),
        name="attn_a",
    )(aq, akb, avb, lp, subln)


def _attn_c_kernel(q_ref, kv_ref, o_ref, *, seq):
    causal = _causal_tile(C_HEADS)
    for qi in range(seq // TQ):
        r0 = qi * TQ
        q4 = jnp.concatenate([q_ref[r0:r0 + TQ, h * 256:(h + 1) * 256] for h in range(C_HEADS)], axis=0)
        s_d = jnp.where(causal, _dot_nt(q4, kv_ref[r0:r0 + TQ, :]) * C_SCALE, NEG_INF)
        s_p = _dot_nt(q4, kv_ref[0:r0, :]) * C_SCALE if qi > 0 else None
        p_p, p_d, l = _softmax_parts(s_p, s_d)
        o = _dot(p_d.astype(BF16), kv_ref[r0:r0 + TQ, 0:C_KV_RANK])
        if qi > 0:
            o = o + _dot(p_p.astype(BF16), kv_ref[0:r0, 0:C_KV_RANK])
        o = o * (1.0 / l)
        for h in range(C_HEADS):
            o_ref[r0:r0 + TQ, h * 128:(h + 1) * 128] = o[h * TQ:(h + 1) * TQ].astype(BF16)


def _attn_c(qc, ckvb, nb, seq):
    return pl.pallas_call(
        functools.partial(_attn_c_kernel, seq=seq),
        out_shape=jax.ShapeDtypeStruct((nb * seq, C_HEADS * C_KV_RANK), BF16),
        grid=(nb,),
        in_specs=[pl.BlockSpec((seq, 1024), lambda n: (n, 0)),
                  pl.BlockSpec((seq, 256), lambda n: (n, 0))],
        out_specs=pl.BlockSpec((seq, C_HEADS * C_KV_RANK), lambda n: (n, 0)),
        compiler_params=_params(("parallel",)),
        name="attn_c",
    )(qc, ckvb)


def _head_masks(rows):
    lane = lax.broadcasted_iota(jnp.int32, (rows, B_HEADS * B_HEAD_DIM), 1)
    return [(lane >= h * B_HEAD_DIM) & (lane < (h + 1) * B_HEAD_DIM) for h in range(B_HEADS)]


def _attn_b_kernel(q_ref, qf_ref, k_ref, kf_ref, v_ref, o_ref, *, seq):
    nblk = seq // MOBA_BLOCK
    hm = _head_masks(TQ)
    causal = _causal_tile(B_HEADS)
    rowblk = lax.broadcasted_iota(jnp.int32, (B_HEADS * TQ, 1), 0) // TQ
    slope = jnp.where(rowblk == 0, SLOPES_B[0],
                      jnp.where(rowblk == 1, SLOPES_B[1],
                                jnp.where(rowblk == 2, SLOPES_B[2], SLOPES_B[3]))).astype(F32)
    rel_d = lax.broadcasted_iota(jnp.int32, (1, TQ), 1).astype(F32)
    kmean = jnp.concatenate(
        [jnp.sum(kf_ref[b * MOBA_BLOCK:(b + 1) * MOBA_BLOCK, :], axis=0, keepdims=True) for b in range(nblk)],
        axis=0) * (1.0 / MOBA_BLOCK)
    blk = lax.broadcasted_iota(jnp.int32, (B_HEADS * TQ, nblk), 1)
    for qi in range(nblk):
        r0 = qi * TQ
        q = q_ref[r0:r0 + TQ, :]
        qz = jnp.concatenate([jnp.where(hm[h], q, jnp.zeros_like(q)) for h in range(B_HEADS)], axis=0)
        s_d = _dot_nt(qz, k_ref[r0:r0 + TQ, :]) + slope * rel_d
        s_d = jnp.where(causal, s_d, NEG_INF)
        s_p = None
        if qi > 0:
            rel_p = (lax.broadcasted_iota(jnp.int32, (1, r0), 1) - r0).astype(F32)
            s_p = _dot_nt(qz, k_ref[0:r0, :]) + slope * rel_p
            if qi > MOBA_TOPK:
                qf = qf_ref[r0:r0 + TQ, :]
                qfz = jnp.concatenate([jnp.where(hm[h], qf, jnp.zeros_like(qf)) for h in range(B_HEADS)],
                                      axis=0)
                gate = _dot_nt(qfz, kmean, precision=lax.Precision.HIGHEST)
                gate = jnp.where(blk < qi, gate, NEG_INF)
                picks = []
                for _ in range(MOBA_TOPK):
                    mx = jnp.max(gate, axis=-1, keepdims=True)
                    first = jnp.min(jnp.where(gate == mx, blk, nblk), axis=-1, keepdims=True)
                    picks.append(first)
                    gate = jnp.where(blk == first, NEG_INF, gate)
                pieces = []
                for b in range(qi):
                    keep = (picks[0] == b) | (picks[1] == b) | (picks[2] == b)
                    pieces.append(jnp.where(keep, s_p[:, b * MOBA_BLOCK:(b + 1) * MOBA_BLOCK], NEG_INF))
                s_p = jnp.concatenate(pieces, axis=1)
        p_p, p_d, l = _softmax_parts(s_p, s_d)
        o = _dot(p_d.astype(BF16), v_ref[r0:r0 + TQ, :])
        if qi > 0:
            o = o + _dot(p_p.astype(BF16), v_ref[0:r0, :])
        o = o * (1.0 / l)
        out = jnp.where(hm[0], o[0:TQ], 0.0)
        for h in range(1, B_HEADS):
            out = out + jnp.where(hm[h], o[h * TQ:(h + 1) * TQ], 0.0)
        o_ref[r0:r0 + TQ, :] = out.astype(BF16)


def _attn_b(bq, bqf, bkb, bk, bvb, nb, seq):
    spec = pl.BlockSpec((seq, 256), lambda n: (n, 0))
    return pl.pallas_call(
        functools.partial(_attn_b_kernel, seq=seq),
        out_shape=jax.ShapeDtypeStruct((nb * seq, 256), BF16),
        grid=(nb,),
        in_specs=[spec, spec, spec, spec, spec],
        out_specs=spec,
        compiler_params=_params(("parallel",)),
        name="attn_b",
    )(bq, bqf, bkb, bk, bvb)


def _merge_kernel(h_ref, oa_ref, ob_ref, oc_ref, wuv_ref, wo_ref, g_ref, o_ref):
    oc = _dot(oc_ref[...], wuv_ref[...]).astype(BF16)
    mix = (_dot(oa_ref[...], wo_ref[0:512, :]) + _dot(ob_ref[...], wo_ref[512:768, :])
           + _dot(oc, wo_ref[768:1024, :]))
    o_ref[...] = h_ref[...] + _rms(mix, g_ref[...])


def _merge(h, oa, ob, oc, wuv, wo, g, tm):
    t = h.shape[0]
    row = lambda w: pl.BlockSpec((tm, w), lambda i: (i, 0))
    return pl.pallas_call(
        _merge_kernel,
        out_shape=jax.ShapeDtypeStruct((t, D_MODEL), F32),
        grid=(t // tm,),
        in_specs=[row(D_MODEL), row(512), row(256), row(512),
                  _const_spec((512, 256)), _const_spec((D_MODEL, D_MODEL)), _const_spec((1, D_MODEL))],
        out_specs=row(D_MODEL),
        compiler_params=_params(("parallel",)),
        name="merge",
    )(h, oa, ob, oc, wuv, wo, g)


PAGES_PER_STEP = 8
N_NEW = 4


def _page_specs(block, layer, pages_per_step):
    def spec(p):
        return pl.BlockSpec((None, None) + block,
                            lambda n, c, pt: (layer, pt[n, c * pages_per_step + p], 0, 0))
    return [spec(p) for p in range(pages_per_step)]


def _online_update(s, v_dot, m_sc, l_sc, acc_sc):
    m_old = m_sc[...][:, 0:1]
    m_new = jnp.maximum(m_old, jnp.max(s, axis=-1, keepdims=True))
    alpha = jnp.exp(m_old - m_new)
    p = jnp.exp(s - m_new)
    l_new = alpha * l_sc[...][:, 0:1] + jnp.sum(p, axis=-1, keepdims=True)
    acc_new = alpha * acc_sc[...] + v_dot(p.astype(BF16))
    m_sc[...] = jnp.broadcast_to(m_new, m_sc.shape)
    l_sc[...] = jnp.broadcast_to(l_new, l_sc.shape)
    acc_sc[...] = acc_new
    return m_new, l_new, acc_new


def _init_online(c, m_sc, l_sc, acc_sc):
    @pl.when(c == 0)
    def _():
        m_sc[...] = jnp.full(m_sc.shape, NEG_INF, F32)
        l_sc[...] = jnp.zeros(l_sc.shape, F32)
        acc_sc[...] = jnp.zeros(acc_sc.shape, F32)


def _dec_a_kernel(pt_ref, q_ref, slope_ref, kn_ref, vn_ref, lp_ref, subln_ref, *rest, lam_init, past_len):
    P = PAGES_PER_STEP
    k_refs, v_refs = rest[0:P], rest[P:2 * P]
    o_ref, m_sc, l_sc, acc_sc = rest[2 * P:]
    c = pl.program_id(1)
    rows = 2 * P * PAGE
    _init_online(c, m_sc, l_sc, acc_sc)

    q = q_ref[...]
    slope = slope_ref[...][:, 0:1]
    row_g = (lax.broadcasted_iota(jnp.int32, (32, 1), 0) // (2 * N_NEW)) % 2
    kc = jnp.concatenate([r[...] for r in k_refs], axis=0).astype(BF16)
    vc = jnp.concatenate([r[...] for r in v_refs], axis=0).astype(BF16)
    col = lax.broadcasted_iota(jnp.int32, (1, rows), 1)
    rel = ((col >> 1) + (c * (P * PAGE) - past_len)).astype(F32)
    s = _dot_nt(q, kc) + slope * rel
    s = jnp.where((col & 1) == row_g, s, NEG_INF)
    m_new, l_new, acc_new = _online_update(s, lambda p: _dot(p, vc), m_sc, l_sc, acc_sc)

    @pl.when(c == pl.num_programs(1) - 1)
    def _():
        lam = _lambda_value(lp_ref[...], lam_init)
        ncol = lax.broadcasted_iota(jnp.int32, (32, 2 * N_NEW), 1)
        q_t = lax.broadcasted_iota(jnp.int32, (32, 2 * N_NEW), 0) % N_NEW
        s_n = _dot_nt(q, kn_ref[...]) + slope * (ncol >> 1).astype(F32)
        s_n = jnp.where(((ncol & 1) == row_g) & ((ncol >> 1) <= q_t), s_n, NEG_INF)
        m_f = jnp.maximum(m_new, jnp.max(s_n, axis=-1, keepdims=True))
        a_f = jnp.exp(m_new - m_f)
        p_n = jnp.exp(s_n - m_f)
        l_f = a_f * l_new + jnp.sum(p_n, axis=-1, keepdims=True)
        acc_f = a_f * acc_new + _dot(p_n.astype(BF16), vn_ref[...])
        o = acc_f * (1.0 / l_f)
        pd = o[0:16] - lam * o[16:32]
        o_ref[...] = (_rms(pd, subln_ref[...]) * (1.0 - lam_init)).astype(BF16)


def _dec_a(pt, qa, slopes, kn, vn, lp, subln, cache_k, cache_v, layer, lam_init, past_len):
    nseq, npages = pt.shape
    P = PAGES_PER_STEP
    per_seq = lambda r, w: pl.BlockSpec((None, r, w), lambda n, c, pt: (n, 0, 0))
    const = lambda r, w: pl.BlockSpec((r, w), lambda n, c, pt: (0, 0))
    grid_spec = pltpu.PrefetchScalarGridSpec(
        num_scalar_prefetch=1,
        grid=(nseq, npages // P),
        in_specs=[per_seq(32, 128), const(32, 128), per_seq(8, 128), per_seq(8, 128),
                  const(4, A_QK_DIM), const(1, A_V_DIM)]
                 + _page_specs((2 * PAGE, 128), layer, P) + _page_specs((2 * PAGE, 128), layer, P),
        out_specs=per_seq(16, A_V_DIM),
        scratch_shapes=[pltpu.VMEM((32, 128), F32), pltpu.VMEM((32, 128), F32), pltpu.VMEM((32, 128), F32)],
    )
    return pl.pallas_call(
        functools.partial(_dec_a_kernel, lam_init=lam_init, past_len=past_len),
        out_shape=jax.ShapeDtypeStruct((nseq, 16, A_V_DIM), BF16),
        grid_spec=grid_spec,
        compiler_params=_params(("parallel", "arbitrary")),
        name="dec_a",
    )(pt, qa, slopes, kn, vn, lp, subln, *([cache_k] * P), *([cache_v] * P))


def _dec_c_kernel(pt_ref, q_ref, kvn_ref, *rest):
    P = PAGES_PER_STEP
    lat_refs, kr_refs = rest[0:P], rest[P:2 * P]
    o_ref, m_sc, l_sc, acc_sc = rest[2 * P:]
    c = pl.program_id(1)
    _init_online(c, m_sc, l_sc, acc_sc)

    q = q_ref[...]
    q_lat = q[:, 0:C_KV_RANK]
    q_rope = q[:, C_KV_RANK:C_KV_RANK + C_ROPE_DIM]
    lat = jnp.concatenate([r[...] for r in lat_refs], axis=0).astype(BF16)
    krt = jnp.concatenate([r[...] for r in kr_refs], axis=1).astype(BF16)
    s = (_dot_nt(q_lat, lat) + _dot(q_rope, krt)) * C_SCALE
    m_new, l_new, acc_new = _online_update(s, lambda p: _dot(p, lat), m_sc, l_sc, acc_sc)

    @pl.when(c == pl.num_programs(1) - 1)
    def _():
        kvn = kvn_ref[...]
        key_t = lax.broadcasted_iota(jnp.int32, (16, 8), 1)
        q_t = lax.broadcasted_iota(jnp.int32, (16, 8), 0) % N_NEW
        s_n = _dot_nt(q, kvn) * C_SCALE
        s_n = jnp.where((key_t <= q_t) & (key_t < N_NEW), s_n, NEG_INF)
        m_f = jnp.maximum(m_new, jnp.max(s_n, axis=-1, keepdims=True))
        a_f = jnp.exp(m_new - m_f)
        p_n = jnp.exp(s_n - m_f)
        l_f = a_f * l_new + jnp.sum(p_n, axis=-1, keepdims=True)
        acc_f = a_f * acc_new + _dot(p_n.astype(BF16), kvn[:, 0:C_KV_RANK])
        o_ref[...] = (acc_f * (1.0 / l_f)).astype(BF16)


def _dec_c(pt, qc, kvn, cache_lat, cache_krt, layer):
    nseq, npages = pt.shape
    P = PAGES_PER_STEP
    per_seq = lambda r, w: pl.BlockSpec((None, r, w), lambda n, c, pt: (n, 0, 0))
    grid_spec = pltpu.PrefetchScalarGridSpec(
        num_scalar_prefetch=1,
        grid=(nseq, npages // P),
        in_specs=[per_seq(16, 256), per_seq(8, 256)]
                 + _page_specs((PAGE, C_KV_RANK), layer, P) + _page_specs((C_ROPE_DIM, PAGE), layer, P),
        out_specs=per_seq(16, C_KV_RANK),
        scratch_shapes=[pltpu.VMEM((16, 128), F32), pltpu.VMEM((16, 128), F32),
                        pltpu.VMEM((16, C_KV_RANK), F32)],
    )
    return pl.pallas_call(
        _dec_c_kernel,
        out_shape=jax.ShapeDtypeStruct((nseq, 16, C_KV_RANK), BF16),
        grid_spec=grid_spec,
        compiler_params=_params(("parallel", "arbitrary")),
        name="dec_c",
    )(pt, qc, kvn, *([cache_lat] * P), *([cache_krt] * P))


def _dec_b_kernel(pt_ref, q_ref, slope_ref, kn_ref, vn_ref, *rest, past_len):
    P = PAGES_PER_STEP
    k_refs, v_refs = rest[0:P], rest[P:2 * P]
    o_ref, m_sc, l_sc, g_sc, o_sc = rest[2 * P:]
    c = pl.program_id(1)
    toks = P * PAGE
    nblk_step = toks // MOBA_BLOCK
    q = q_ref[...]
    slope = slope_ref[...][:, 0:1]
    kt = jnp.concatenate([r[...] for r in k_refs], axis=1).astype(BF16)
    vt = jnp.concatenate([r[...] for r in v_refs], axis=1).astype(BF16)
    s_raw = _dot(q, kt)
    rel = (lax.broadcasted_iota(jnp.int32, (1, toks), 1) + (c * toks - past_len)).astype(F32)
    s = s_raw + slope * rel
    for j in range(nblk_step):
        blk_id = c * nblk_step + j
        lo = j * MOBA_BLOCK
        sj = s[:, lo:lo + MOBA_BLOCK]
        m_b = jnp.max(sj, axis=-1, keepdims=True)
        p = jnp.exp(sj - m_b)
        l_b = jnp.sum(p, axis=-1, keepdims=True)
        gate = jnp.sum(s_raw[:, lo:lo + MOBA_BLOCK], axis=-1, keepdims=True)
        m_sc[blk_id] = jnp.broadcast_to(m_b, (16, 128))
        l_sc[blk_id] = jnp.broadcast_to(l_b, (16, 128))
        g_sc[blk_id] = jnp.broadcast_to(gate, (16, 128))
        o_sc[blk_id] = _dot_nt(p.astype(BF16), vt[:, lo:lo + MOBA_BLOCK])

    @pl.when(c == pl.num_programs(1) - 1)
    def _():
        nblk = m_sc.shape[0]
        gate = g_sc[...]
        blk = lax.broadcasted_iota(jnp.int32, gate.shape, 0)
        sel = jnp.zeros(gate.shape, F32)
        for _ in range(min(MOBA_TOPK, nblk)):
            mx = jnp.max(gate, axis=0, keepdims=True)
            first = jnp.min(jnp.where(gate == mx, blk, nblk), axis=0, keepdims=True)
            pick = blk == first
            sel = jnp.where(pick, 1.0, sel)
            gate = jnp.where(pick, NEG_INF, gate)
        keep = sel > 0.5
        key_t = lax.broadcasted_iota(jnp.int32, (16, 8), 1)
        q_t = lax.broadcasted_iota(jnp.int32, (16, 8), 0) % N_NEW
        s_n = _dot_nt(q, kn_ref[...]) + slope * key_t.astype(F32)
        s_n = jnp.where((key_t <= q_t) & (key_t < N_NEW), s_n, NEG_INF)
        m_n = jnp.max(s_n, axis=-1, keepdims=True)
        m_all = jnp.where(keep, m_sc[...], NEG_INF)
        m_f = jnp.maximum(jnp.max(m_all, axis=0), m_n)
        w = jnp.where(keep, jnp.exp(m_sc[...] - m_f[None]), 0.0)
        p_n = jnp.exp(s_n - m_f[:, 0:1])
        l_f = jnp.sum(w * l_sc[...], axis=0)[:, 0:1] + jnp.sum(p_n, axis=-1, keepdims=True)
        acc = _dot(p_n.astype(BF16), vn_ref[...])
        acc = acc + jnp.sum(jnp.concatenate([w, w], axis=-1) * o_sc[...], axis=0)
        o = acc * (1.0 / l_f)
        rh = lax.broadcasted_iota(jnp.int32, (16, 256), 0) // N_NEW
        ch = lax.broadcasted_iota(jnp.int32, (16, 256), 1) // B_HEAD_DIM
        o_ref[...] = jnp.where(rh == ch, o, 0.0).astype(BF16)


def _dec_b(pt, qbd, slopes, kn, vn, cache_kt, cache_vt, layer, past_len):
    nseq, npages = pt.shape
    P = PAGES_PER_STEP
    nblk = past_len // MOBA_BLOCK
    per_seq = lambda r, w: pl.BlockSpec((None, r, w), lambda n, c, pt: (n, 0, 0))
    const = lambda r, w: pl.BlockSpec((r, w), lambda n, c, pt: (0, 0))
    grid_spec = pltpu.PrefetchScalarGridSpec(
        num_scalar_prefetch=1,
        grid=(nseq, npages // P),
        in_specs=[per_seq(16, 256), const(16, 128), per_seq(8, 256), per_seq(8, 256)]
                 + _page_specs((256, PAGE), layer, P) + _page_specs((256, PAGE), layer, P),
        out_specs=per_seq(16, 256),
        scratch_shapes=[pltpu.VMEM((nblk, 16, 128), F32), pltpu.VMEM((nblk, 16, 128), F32),
                        pltpu.VMEM((nblk, 16, 128), F32), pltpu.VMEM((nblk, 16, 256), F32)],
    )
    return pl.pallas_call(
        functools.partial(_dec_b_kernel, past_len=past_len),
        out_shape=jax.ShapeDtypeStruct((nseq, 16, 256), BF16),
        grid_spec=grid_spec,
        compiler_params=_params(("parallel", "arbitrary")),
        name="dec_b",
    )(pt, qbd, slopes, kn, vn, *([cache_kt] * P), *([cache_vt] * P))


def _rope_tables(pos):
    half = C_ROPE_DIM // 2
    inv = ROPE_BASE ** (-jnp.arange(half, dtype=F32) / half)
    ang = pos.astype(F32)[:, None] * inv[None, :]
    cos, sin = jnp.cos(ang), jnp.sin(ang)
    qcs = jnp.tile(cos, (1, 8))
    qsn = jnp.concatenate([jnp.tile(-sin, (1, 4)), jnp.tile(sin, (1, 4))], axis=1)
    pad = jnp.zeros((pos.shape[0], 128 - C_ROPE_DIM), F32)
    kcs = jnp.concatenate([cos, cos, pad], axis=1)
    ksn = jnp.concatenate([-sin, sin, pad], axis=1)
    return qcs, qsn, kcs, ksn


def _uq_perm():
    per = C_NOPE_DIM + C_ROPE_DIM
    half = C_ROPE_DIM // 2
    nope = [h * per + d for h in range(C_HEADS) for d in range(C_NOPE_DIM)]
    r1 = [h * per + C_NOPE_DIM + j for h in range(C_HEADS) for j in range(half)]
    r2 = [h * per + C_NOPE_DIM + half + j for h in range(C_HEADS) for j in range(half)]
    return np.array(nope + r1 + r2, dtype=np.int32)


def _rope_placement():
    half = C_ROPE_DIM // 2
    pm = np.zeros((128, C_HEADS * 256), dtype=np.float32)
    for h in range(C_HEADS):
        for j in range(half):
            pm[h * half + j, h * 256 + C_KV_RANK + j] = 1.0
            pm[64 + h * half + j, h * 256 + C_KV_RANK + half + j] = 1.0
    return pm


def _layer_weights(l, norm_gains, ffn_w_gate_up, ffn_w_down, w_in, w_out, diff_subln,
                   mla_q_norm, mla_w_uq, mla_kv_norm, mla_w_ukv):
    eye = jnp.eye(C_HEADS, dtype=F32)
    w_ukv = mla_w_ukv[l].reshape(C_KV_RANK, C_HEADS, C_NOPE_DIM + C_V_DIM)
    w_uk = w_ukv[..., :C_NOPE_DIM]
    w_uv = w_ukv[..., C_NOPE_DIM:]
    uk_bd = jnp.einsum('chd,hg->hdgc', w_uk, eye)
    uk_bd = jnp.pad(uk_bd, ((0, 0), (0, 0), (0, 0), (0, 256 - C_KV_RANK))).reshape(256, C_HEADS * 256)
    wqc = jnp.concatenate([uk_bd, jnp.asarray(_rope_placement())], axis=0).astype(BF16)
    wuv = jnp.einsum('chv,hg->hcgv', w_uv, eye).reshape(C_HEADS * C_KV_RANK, C_HEADS * C_V_DIM).astype(BF16)
    g = norm_gains[l]
    return dict(
        gains=[g[i][None, :] for i in range(6)],
        wgu=ffn_w_gate_up[l].astype(BF16), wdn=ffn_w_down[l].astype(BF16),
        win=jnp.pad(w_in[l], ((0, 0), (0, IN_PAD - IN_COLS))).astype(BF16),
        wo=w_out[l].astype(BF16),
        subln=diff_subln[l][None, :],
        qn=mla_q_norm[l][None, :], kvn=mla_kv_norm[l][None, :],
        wuq=mla_w_uq[l][:, _uq_perm()].astype(BF16), wqc=wqc, wuv=wuv)


def kernel(x_prompt, x_sample, cache_a_k, cache_a_v, cache_b_k, cache_b_v, cache_c_latent, cache_c_krope,
           page_table, norm_gains, ffn_w_gate_up, ffn_w_down, w_in, w_out, diff_lambda, diff_subln,
           mla_q_norm, mla_w_uq, mla_kv_norm, mla_w_ukv):
    nb, seq, _ = x_prompt.shape
    ns, n_new, _ = x_sample.shape
    depth = w_in.shape[0]
    n_pool, page = cache_a_k.shape[1], cache_a_k.shape[2]
    past_len = page_table.shape[1] * page
    tp, ts = nb * seq, ns * n_new
    assert seq % TQ == 0 and TQ == MOBA_BLOCK and page == PAGE and past_len % MOBA_BLOCK == 0
    assert page_table.shape[1] % PAGES_PER_STEP == 0 and n_new == N_NEW

    tabs_p = _rope_tables(jnp.arange(seq, dtype=jnp.int32))
    tabs_s = _rope_tables(jnp.tile(past_len + jnp.arange(n_new, dtype=jnp.int32), ns))
    ca_k = cache_a_k.reshape(depth, n_pool, 2 * page, 128)
    ca_v = cache_a_v.reshape(depth, n_pool, 2 * page, 128)
    cb_kt = cache_b_k.transpose(0, 1, 3, 4, 2).reshape(depth, n_pool, 256, page)
    cb_vt = cache_b_v.transpose(0, 1, 3, 4, 2).reshape(depth, n_pool, 256, page)
    cc_krt = cache_c_krope.transpose(0, 1, 3, 2)

    sl_a = np.array([SLOPES_A[2 * g + r] for _ in range(2) for g in range(2) for r in range(2)
                     for _ in range(n_new)], dtype=np.float32)
    sl_b = np.array([SLOPES_B[h] for h in range(B_HEADS) for _ in range(n_new)], dtype=np.float32)
    sl_a = jnp.asarray(np.tile(sl_a[:, None], (1, 128)))
    sl_b = jnp.asarray(np.tile(sl_b[:, None], (1, 128)))
    eye2 = jnp.eye(2, dtype=BF16)
    eye4 = jnp.eye(4, dtype=BF16)

    def pad_new(x):
        x = x.reshape(ns, n_new, x.shape[-1])
        return jnp.pad(x, ((0, 0), (0, 8 - n_new), (0, 0)))

    h_p = x_prompt.reshape(tp, D_MODEL)
    h_s = x_sample.reshape(ts, D_MODEL)
    rows_p, rows_s = [], []
    for l in range(depth):
        lam_init = 0.8 - 0.6 * math.exp(-0.3 * l)
        w = _layer_weights(l, norm_gains, ffn_w_gate_up, ffn_w_down, w_in, w_out, diff_subln,
                           mla_q_norm, mla_w_uq, mla_kv_norm, mla_w_ukv)
        gn = w['gains']
        lp = diff_lambda[l]

        h1 = _ffn(h_p, gn[0], gn[1], w['wgu'][0], w['wdn'][0], 512)
        (aq, ak, av, bq, bqf, bk, bv, qc, clat, ckr, akb, avb, bkb, bvb, ckvb) = _proj(
            h1, gn[2], w['win'], w['qn'], w['wuq'], w['wqc'], w['kvn'], tabs_p, 512, seq // 512)
        oa = _attn_a(aq, akb, avb, lp, w['subln'], lam_init, nb, seq)
        ob = _attn_b(bq, bqf, bkb, bk, bvb, nb, seq)
        oc = _attn_c(qc, ckvb, nb, seq)
        h2 = _merge(h1, oa, ob, oc, w['wuv'], w['wo'], gn[3], 512)
        h_p = _ffn(h2, gn[4], gn[5], w['wgu'][1], w['wdn'][1], 512)
        rows_p.append((ak.reshape(nb, seq, A_KV_HEADS, 2 * A_QK_DIM), av.reshape(nb, seq, A_KV_HEADS, A_V_DIM),
                       bk.reshape(nb, seq, B_HEADS, B_HEAD_DIM), bv.reshape(nb, seq, B_HEADS, B_HEAD_DIM),
                       clat.reshape(nb, seq, C_KV_RANK), ckr.reshape(nb, seq, C_ROPE_DIM)))

        g1 = _ffn(h_s, gn[0], gn[1], w['wgu'][0], w['wdn'][0], ts)
        (aq, ak, av, bq, bqf, bk, bv, qc, clat, ckr, akb, avb, bkb, bvb, ckvb) = _proj(
            g1, gn[2], w['win'], w['qn'], w['wuq'], w['wqc'], w['kvn'], tabs_s, ts, 1)
        qa = aq.reshape(ns, n_new, 2, 2, 2, A_QK_DIM).transpose(0, 4, 2, 3, 1, 5)
        qa = jnp.einsum('nmgrtd,mM->nmgrtMd', qa, eye2).reshape(ns, 8 * n_new, 128)
        qb = bq.reshape(ns, n_new, B_HEADS, B_HEAD_DIM).transpose(0, 2, 1, 3)
        qb = jnp.einsum('nhtd,hH->nhtHd', qb, eye4).reshape(ns, 4 * n_new, 256)
        qcs = qc.reshape(ns, n_new, C_HEADS, 256).transpose(0, 2, 1, 3).reshape(ns, 4 * n_new, 256)
        oa = _dec_a(page_table, qa, sl_a, akb.reshape(ns, 2 * n_new, 128), avb.reshape(ns, 2 * n_new, 128),
                    lp, w['subln'], ca_k, ca_v, l, lam_init, past_len)
        ob = _dec_b(page_table, qb, sl_b, pad_new(bkb), pad_new(bvb), cb_kt, cb_vt, l, past_len)
        oc = _dec_c(page_table, qcs, pad_new(ckvb), cache_c_latent, cc_krt, l)
        oa = oa.reshape(ns, A_HEADS, n_new, A_V_DIM).transpose(0, 2, 1, 3).reshape(ts, 512)
        ob = ob.reshape(ns, B_HEADS, n_new, B_HEADS, B_HEAD_DIM)
        ob = jnp.stack([ob[:, hh, :, hh] for hh in range(B_HEADS)], axis=1)
        ob = ob.transpose(0, 2, 1, 3).reshape(ts, 256)
        oc = oc.reshape(ns, C_HEADS, n_new, C_KV_RANK).transpose(0, 2, 1, 3).reshape(ts, 512)
        g2 = _merge(g1, oa, ob, oc, w['wuv'], w['wo'], gn[3], ts)
        h_s = _ffn(g2, gn[4], gn[5], w['wgu'][1], w['wdn'][1], ts)
        rows_s.append((ak.reshape(ns, n_new, A_KV_HEADS, 2 * A_QK_DIM), av.reshape(ns, n_new, A_KV_HEADS, A_V_DIM),
                       bk.reshape(ns, n_new, B_HEADS, B_HEAD_DIM), bv.reshape(ns, n_new, B_HEADS, B_HEAD_DIM),
                       clat.reshape(ns, n_new, C_KV_RANK), ckr.reshape(ns, n_new, C_ROPE_DIM)))

    stack = lambda rows, i: jnp.stack([r[i] for r in rows])
    return (h_p.reshape(nb, seq, D_MODEL), h_s.reshape(ns, n_new, D_MODEL),
            *[stack(rows_p, i) for i in range(6)], *[stack(rows_s, i) for i in range(6)])
```

```python
import functools
import math

import numpy as np
import jax
import jax.numpy as jnp
from jax import lax
from jax.experimental import pallas as pl
from jax.experimental.pallas import tpu as pltpu

F32 = jnp.float32
BF16 = jnp.bfloat16

D_MODEL = 1024
D_FF = 2816
A_HEADS = 4
A_KV_HEADS = 2
A_QK_DIM = 64
A_V_DIM = 128
B_HEADS = 4
B_HEAD_DIM = 64
MOBA_BLOCK = 256
MOBA_TOPK = 3
C_HEADS = 4
C_Q_RANK = 256
C_KV_RANK = 128
C_NOPE_DIM = 64
C_ROPE_DIM = 32
C_V_DIM = 64
C_SCALE = (C_NOPE_DIM + C_ROPE_DIM) ** -0.5
ROPE_BASE = 10000.0
NORM_EPS = 1e-6
QK_SCALE = 0.125

IN_COLS = 2208
IN_PAD = 2304
TQ = 256
PAGE = 128
NEG_INF = float("-inf")

VMEM_LIMIT = 56 * 1024 * 1024

SLOPES_A = tuple(2.0 ** -(2 * h + 1) for h in range(A_HEADS))
SLOPES_B = tuple(2.0 ** -(2 * h + 2) for h in range(B_HEADS))


def _rms(x, g):
    return x * lax.rsqrt(jnp.mean(x * x, axis=-1, keepdims=True) + NORM_EPS) * g


def _dot(a, b):
    return jnp.dot(a, b, preferred_element_type=F32)


def _dot_nt(a, b, precision=None):
    return lax.dot_general(a, b, (((1,), (1,)), ((), ())), preferred_element_type=F32,
                           precision=precision)


def _const_spec(shape):
    nd = len(shape)
    return pl.BlockSpec(shape, lambda *_: (0,) * nd, pipeline_mode=pl.Buffered(1))


def _params(sem):
    return pltpu.CompilerParams(dimension_semantics=sem, vmem_limit_bytes=VMEM_LIMIT)


FFN_CHUNKS = 2
FFN_CHUNK = D_FF // FFN_CHUNKS


def _ffn_body(x, gpre, gpost, wgu_ref, wdn_ref):
    xn = _rms(x, gpre).astype(BF16)
    acc = None
    for c in range(FFN_CHUNKS):
        lo = c * FFN_CHUNK
        g = _dot(xn, wgu_ref[:, lo:lo + FFN_CHUNK])
        u = _dot(xn, wgu_ref[:, D_FF + lo:D_FF + lo + FFN_CHUNK])
        a = (g * (1.0 / (1.0 + jnp.exp(-g))) * u).astype(BF16)
        part = _dot(a, wdn_ref[lo:lo + FFN_CHUNK, :])
        acc = part if acc is None else acc + part
    return x + 0.5 * _rms(acc, gpost)


def _ffn_kernel(x_ref, gpre_ref, gpost_ref, wgu_ref, wdn_ref, o_ref):
    o_ref[...] = _ffn_body(x_ref[...], gpre_ref[...], gpost_ref[...], wgu_ref, wdn_ref)


def _ffn(x, gpre, gpost, wgu, wdn, tm):
    t = x.shape[0]
    return pl.pallas_call(
        _ffn_kernel,
        out_shape=jax.ShapeDtypeStruct((t, D_MODEL), F32),
        grid=(t // tm,),
        in_specs=[pl.BlockSpec((tm, D_MODEL), lambda i: (i, 0)),
                  _const_spec((1, D_MODEL)), _const_spec((1, D_MODEL)),
                  _const_spec((D_MODEL, 2 * D_FF)), _const_spec((D_FF, D_MODEL))],
        out_specs=pl.BlockSpec((tm, D_MODEL), lambda i: (i, 0)),
        compiler_params=_params(("parallel",)),
        name="ffn",
    )(x, gpre, gpost, wgu, wdn)


def _proj_kernel(h_ref, g_ref, win_ref, qn_ref, wuq_ref, wqc_ref, kvn_ref,
                 qcs_ref, qsn_ref, kcs_ref, ksn_ref,
                 aq_ref, ak_ref, av_ref, bq_ref, bqf_ref, bk_ref, bv_ref, qc_ref, clat_ref, ckr_ref,
                 akb_ref, avb_ref, bkb_ref, bvb_ref, ckvb_ref):
    u = _rms(h_ref[...], g_ref[...]).astype(BF16)
    z = _dot(u, win_ref[...])
    aq_ref[...] = (z[:, 0:512] * QK_SCALE).astype(BF16)
    ak = z[:, 512:768]
    ak_ref[...] = ak
    akb_ref[...] = ak.astype(BF16)
    av = z[:, 768:1024]
    av_ref[...] = av
    avb_ref[...] = av.astype(BF16)
    bq = z[:, 1024:1280]
    bqf_ref[...] = bq
    bq_ref[...] = (bq * QK_SCALE).astype(BF16)
    bk = z[:, 1280:1536]
    bk_ref[...] = bk
    bkb_ref[...] = bk.astype(BF16)
    bv = z[:, 1536:1792]
    bv_ref[...] = bv
    bvb_ref[...] = bv.astype(BF16)
    cqn = _rms(z[:, 1792:2048], qn_ref[...]).astype(BF16)
    q = _dot(cqn, wuq_ref[...])
    qr = q[:, 256:384]
    qrot = qr * qcs_ref[...] + pltpu.roll(qr, 64, 1) * qsn_ref[...]
    qcat = jnp.concatenate([q[:, 0:256], qrot], axis=1).astype(BF16)
    qc_ref[...] = _dot(qcat, wqc_ref[...]).astype(BF16)
    clat = _rms(z[:, 2048:2176], kvn_ref[...])
    clat_ref[...] = clat
    kr = z[:, 2176:2304]
    lane = lax.broadcasted_iota(jnp.int32, kr.shape, 1)
    swapped = jnp.where(lane < 16, pltpu.roll(kr, 112, 1), pltpu.roll(kr, 16, 1))
    krot = kr * kcs_ref[...] + swapped * ksn_ref[...]
    ckr_ref[...] = krot[:, 0:C_ROPE_DIM]
    ckvb_ref[...] = jnp.concatenate([clat, krot], axis=1).astype(BF16)


def _proj(h, g, win, qn, wuq, wqc, kvn, tabs, tm, tab_blocks):
    t = h.shape[0]
    row = lambda w: pl.BlockSpec((tm, w), lambda i: (i, 0))
    tab = pl.BlockSpec((tm, 128), lambda i: (i % tab_blocks, 0))
    shapes = [(512, BF16), (256, F32), (256, F32), (256, BF16), (256, F32), (256, F32), (256, F32),
              (1024, BF16), (128, F32), (C_ROPE_DIM, F32),
              (256, BF16), (256, BF16), (256, BF16), (256, BF16), (256, BF16)]
    return pl.pallas_call(
        _proj_kernel,
        out_shape=[jax.ShapeDtypeStruct((t, w), d) for w, d in shapes],
        grid=(t // tm,),
        in_specs=[row(D_MODEL), _const_spec((1, D_MODEL)), _const_spec((D_MODEL, IN_PAD)),
                  _const_spec((1, C_Q_RANK)), _const_spec((C_Q_RANK, 384)), _const_spec((384, 1024)),
                  _const_spec((1, C_KV_RANK)), tab, tab, tab, tab],
        out_specs=[row(w) for w, _ in shapes],
        compiler_params=_params(("parallel",)),
        name="proj",
    )(h, g, win, qn, wuq, wqc, kvn, *tabs)


def _lambda_value(lp, lam_init):
    return (jnp.exp(jnp.sum(lp[0:1, :] * lp[1:2, :], axis=-1, keepdims=True))
            - jnp.exp(jnp.sum(lp[2:3, :] * lp[3:4, :], axis=-1, keepdims=True)) + lam_init)


def _causal_tile(reps):
    row = lax.broadcasted_iota(jnp.int32, (TQ, TQ), 0)
    col = lax.broadcasted_iota(jnp.int32, (TQ, TQ), 1)
    m = col <= row
    return jnp.concatenate([m] * reps, axis=0)


def _softmax_parts(s_p, s_d):
    m = jnp.max(s_d, axis=-1, keepdims=True)
    if s_p is not None:
        m = jnp.maximum(m, jnp.max(s_p, axis=-1, keepdims=True))
    p_d = jnp.exp(s_d - m)
    l = jnp.sum(p_d, axis=-1, keepdims=True)
    p_p = None
    if s_p is not None:
        p_p = jnp.exp(s_p - m)
        l = l + jnp.sum(p_p, axis=-1, keepdims=True)
    return p_p, p_d, l


def _attn_a_kernel(q_ref, k_ref, v_ref, lp_ref, subln_ref, o_ref, *, lam_init, seq):
    g = pl.program_id(1)
    lam = _lambda_value(lp_ref[...], lam_init)
    lane = lax.broadcasted_iota(jnp.int32, (TQ, 128), 1)
    is_m0 = lane < A_QK_DIM
    causal = _causal_tile(4)
    rowblk = lax.broadcasted_iota(jnp.int32, (4 * TQ, 1), 0) // TQ
    s_r0 = jnp.where(g == 0, SLOPES_A[0], SLOPES_A[2])
    s_r1 = jnp.where(g == 0, SLOPES_A[1], SLOPES_A[3])
    slope = jnp.where(rowblk % 2 == 0, s_r0, s_r1).astype(F32)
    coef_sign = jnp.where(rowblk < 2, 1.0, -lam)
    rel_d = lax.broadcasted_iota(jnp.int32, (1, TQ), 1).astype(F32)
    for qi in range(seq // TQ):
        r0 = qi * TQ
        q0 = q_ref[r0:r0 + TQ, 0:128]
        q1 = q_ref[r0:r0 + TQ, 128:256]
        zero = jnp.zeros_like(q0)
        qz = jnp.concatenate([jnp.where(is_m0, q0, zero), jnp.where(is_m0, q1, zero),
                              jnp.where(is_m0, zero, q0), jnp.where(is_m0, zero, q1)], axis=0)
        s_d = _dot_nt(qz, k_ref[r0:r0 + TQ, :]) + slope * rel_d
        s_d = jnp.where(causal, s_d, NEG_INF)
        s_p = None
        if qi > 0:
            rel_p = (lax.broadcasted_iota(jnp.int32, (1, r0), 1) - r0).astype(F32)
            s_p = _dot_nt(qz, k_ref[0:r0, :]) + slope * rel_p
        p_p, p_d, l = _softmax_parts(s_p, s_d)
        coef = coef_sign / l
        w_d = p_d * coef
        pd_d = (w_d[0:2 * TQ] + w_d[2 * TQ:4 * TQ]).astype(BF16)
        o = _dot(pd_d, v_ref[r0:r0 + TQ, :])
        if qi > 0:
            w_p = p_p * coef
            pd_p = (w_p[0:2 * TQ] + w_p[2 * TQ:4 * TQ]).astype(BF16)
            o = o + _dot(pd_p, v_ref[0:r0, :])
        y = _rms(o, subln_ref[...]) * (1.0 - lam_init)
        o_ref[r0:r0 + TQ, 0:128] = y[0:TQ].astype(BF16)
        o_ref[r0:r0 + TQ, 128:256] = y[TQ:2 * TQ].astype(BF16)


def _attn_a(aq, akb, avb, lp, subln, lam_init, nb, seq):
    return pl.pallas_call(
        functools.partial(_attn_a_kernel, lam_init=lam_init, seq=seq),
        out_shape=jax.ShapeDtypeStruct((nb * seq, 512), BF16),
        grid=(nb, A_KV_HEADS),
        in_specs=[pl.BlockSpec((seq, 256), lambda n, g: (n, g)),
                  pl.BlockSpec((seq, 128), lambda n, g: (n, g)),
                  pl.BlockSpec((seq, 128), lambda n, g: (n, g)),
                  _const_spec((4, A_QK_DIM)), _const_spec((1, A_V_DIM))],
        out_specs=pl.BlockSpec((seq, 256), lambda n, g: (n, g)),
        compiler_params=_params(("parallel", "parallel")),
        name="attn_a",
    )(aq, akb, avb, lp, subln)


def _attn_c_kernel(q_ref, kv_ref, o_ref, *, seq):
    causal = _causal_tile(C_HEADS)
    for qi in range(seq // TQ):
        r0 = qi * TQ
        q4 = jnp.concatenate([q_ref[r0:r0 + TQ, h * 256:(h + 1) * 256] for h in range(C_HEADS)], axis=0)
        s_d = jnp.where(causal, _dot_nt(q4, kv_ref[r0:r0 + TQ, :]) * C_SCALE, NEG_INF)
        s_p = _dot_nt(q4, kv_ref[0:r0, :]) * C_SCALE if qi > 0 else None
        p_p, p_d, l = _softmax_parts(s_p, s_d)
        o = _dot(p_d.astype(BF16), kv_ref[r0:r0 + TQ, 0:C_KV_RANK])
        if qi > 0:
            o = o + _dot(p_p.astype(BF16), kv_ref[0:r0, 0:C_KV_RANK])
        o = o * (1.0 / l)
        for h in range(C_HEADS):
            o_ref[r0:r0 + TQ, h * 128:(h + 1) * 128] = o[h * TQ:(h + 1) * TQ].astype(BF16)


def _attn_c(qc, ckvb, nb, seq):
    return pl.pallas_call(
        functools.partial(_attn_c_kernel, seq=seq),
        out_shape=jax.ShapeDtypeStruct((nb * seq, C_HEADS * C_KV_RANK), BF16),
        grid=(nb,),
        in_specs=[pl.BlockSpec((seq, 1024), lambda n: (n, 0)),
                  pl.BlockSpec((seq, 256), lambda n: (n, 0))],
        out_specs=pl.BlockSpec((seq, C_HEADS * C_KV_RANK), lambda n: (n, 0)),
        compiler_params=_params(("parallel",)),
        name="attn_c",
    )(qc, ckvb)


def _head_masks(rows):
    lane = lax.broadcasted_iota(jnp.int32, (rows, B_HEADS * B_HEAD_DIM), 1)
    return [(lane >= h * B_HEAD_DIM) & (lane < (h + 1) * B_HEAD_DIM) for h in range(B_HEADS)]


def _attn_b_kernel(q_ref, qf_ref, k_ref, kf_ref, v_ref, o_ref, *, seq):
    nblk = seq // MOBA_BLOCK
    hm = _head_masks(TQ)
    causal = _causal_tile(B_HEADS)
    rowblk = lax.broadcasted_iota(jnp.int32, (B_HEADS * TQ, 1), 0) // TQ
    slope = jnp.where(rowblk == 0, SLOPES_B[0],
                      jnp.where(rowblk == 1, SLOPES_B[1],
                                jnp.where(rowblk == 2, SLOPES_B[2], SLOPES_B[3]))).astype(F32)
    rel_d = lax.broadcasted_iota(jnp.int32, (1, TQ), 1).astype(F32)
    kmean = jnp.concatenate(
        [jnp.sum(kf_ref[b * MOBA_BLOCK:(b + 1) * MOBA_BLOCK, :], axis=0, keepdims=True) for b in range(nblk)],
        axis=0) * (1.0 / MOBA_BLOCK)
    blk = lax.broadcasted_iota(jnp.int32, (B_HEADS * TQ, nblk), 1)
    for qi in range(nblk):
        r0 = qi * TQ
        q = q_ref[r0:r0 + TQ, :]
        qz = jnp.concatenate([jnp.where(hm[h], q, jnp.zeros_like(q)) for h in range(B_HEADS)], axis=0)
        s_d = _dot_nt(qz, k_ref[r0:r0 + TQ, :]) + slope * rel_d
        s_d = jnp.where(causal, s_d, NEG_INF)
        s_p = None
        if qi > 0:
            rel_p = (lax.broadcasted_iota(jnp.int32, (1, r0), 1) - r0).astype(F32)
            s_p = _dot_nt(qz, k_ref[0:r0, :]) + slope * rel_p
            if qi > MOBA_TOPK:
                qf = qf_ref[r0:r0 + TQ, :]
                qfz = jnp.concatenate([jnp.where(hm[h], qf, jnp.zeros_like(qf)) for h in range(B_HEADS)],
                                      axis=0)
                gate = _dot_nt(qfz, kmean, precision=lax.Precision.HIGHEST)
                gate = jnp.where(blk < qi, gate, NEG_INF)
                picks = []
                for _ in range(MOBA_TOPK):
                    mx = jnp.max(gate, axis=-1, keepdims=True)
                    first = jnp.min(jnp.where(gate == mx, blk, nblk), axis=-1, keepdims=True)
                    picks.append(first)
                    gate = jnp.where(blk == first, NEG_INF, gate)
                pieces = []
                for b in range(qi):
                    keep = (picks[0] == b) | (picks[1] == b) | (picks[2] == b)
                    pieces.append(jnp.where(keep, s_p[:, b * MOBA_BLOCK:(b + 1) * MOBA_BLOCK], NEG_INF))
                s_p = jnp.concatenate(pieces, axis=1)
        p_p, p_d, l = _softmax_parts(s_p, s_d)
        o = _dot(p_d.astype(BF16), v_ref[r0:r0 + TQ, :])
        if qi > 0:
            o = o + _dot(p_p.astype(BF16), v_ref[0:r0, :])
        o = o * (1.0 / l)
        out = jnp.where(hm[0], o[0:TQ], 0.0)
        for h in range(1, B_HEADS):
            out = out + jnp.where(hm[h], o[h * TQ:(h + 1) * TQ], 0.0)
        o_ref[r0:r0 + TQ, :] = out.astype(BF16)


def _attn_b(bq, bqf, bkb, bk, bvb, nb, seq):
    spec = pl.BlockSpec((seq, 256), lambda n: (n, 0))
    return pl.pallas_call(
        functools.partial(_attn_b_kernel, seq=seq),
        out_shape=jax.ShapeDtypeStruct((nb * seq, 256), BF16),
        grid=(nb,),
        in_specs=[spec, spec, spec, spec, spec],
        out_specs=spec,
        compiler_params=_params(("parallel",)),
        name="attn_b",
    )(bq, bqf, bkb, bk, bvb)


def _merge_kernel(h_ref, oa_ref, ob_ref, oc_ref, wuv_ref, wo_ref, g_ref, o_ref):
    oc = _dot(oc_ref[...], wuv_ref[...]).astype(BF16)
    mix = (_dot(oa_ref[...], wo_ref[0:512, :]) + _dot(ob_ref[...], wo_ref[512:768, :])
           + _dot(oc, wo_ref[768:1024, :]))
    o_ref[...] = h_ref[...] + _rms(mix, g_ref[...])


def _merge(h, oa, ob, oc, wuv, wo, g, tm):
    t = h.shape[0]
    row = lambda w: pl.BlockSpec((tm, w), lambda i: (i, 0))
    return pl.pallas_call(
        _merge_kernel,
        out_shape=jax.ShapeDtypeStruct((t, D_MODEL), F32),
        grid=(t // tm,),
        in_specs=[row(D_MODEL), row(512), row(256), row(512),
                  _const_spec((512, 256)), _const_spec((D_MODEL, D_MODEL)), _const_spec((1, D_MODEL))],
        out_specs=row(D_MODEL),
        compiler_params=_params(("parallel",)),
        name="merge",
    )(h, oa, ob, oc, wuv, wo, g)


N_NEW = 4
DEC_VMEM_LIMIT = 58 * 1024 * 1024
ISSUE_UNROLL = 8


def _page_copy(pt_ref, seq, j, layer, cache_ref, buf_ref, slot, sem, stack_rows):
    src = cache_ref.at[layer, pt_ref[seq, j]]
    if stack_rows:
        r = cache_ref.shape[2]
        dst = buf_ref.at[slot, pl.ds(pl.multiple_of(j * r, r), r), :]
    else:
        dst = buf_ref.at[slot, :, pl.ds(pl.multiple_of(j * PAGE, PAGE), PAGE)]
    return pltpu.make_async_copy(src, dst, sem)


def _paged_fetch(pt_ref, layer, streams, npages):
    n = pl.program_id(0)
    slot = n % 2

    def for_all_pages(seq, sl, act):
        def body(j, carry):
            for cache_ref, buf_ref, sem_ref, stack_rows in streams:
                act(_page_copy(pt_ref, seq, j, layer, cache_ref, buf_ref, sl, sem_ref.at[sl], stack_rows))
            return carry
        lax.fori_loop(0, npages, body, 0, unroll=ISSUE_UNROLL)

    @pl.when(n == 0)
    def _():
        for_all_pages(0, 0, lambda cp: cp.start())

    @pl.when(n + 1 < pl.num_programs(0))
    def _():
        for_all_pages(n + 1, 1 - slot, lambda cp: cp.start())

    for_all_pages(n, slot, lambda cp: cp.wait())
    return slot


def _dec_a_kernel(pt_ref, q_ref, kn_ref, vn_ref, slope_ref, lp_ref, subln_ref, ck_ref, cv_ref,
                  o_ref, kbuf, vbuf, ksem, vsem, *, layer, lam_init, past_len):
    npages = past_len // PAGE
    slot = _paged_fetch(pt_ref, layer, [(ck_ref, kbuf, ksem, True), (cv_ref, vbuf, vsem, True)], npages)
    rows = 2 * past_len
    q = q_ref[...]
    slope = slope_ref[...][:, 0:1]
    row_g = (lax.broadcasted_iota(jnp.int32, (32, 1), 0) // (2 * N_NEW)) % 2
    kc = kbuf[slot].astype(BF16)
    vc = vbuf[slot].astype(BF16)
    col = lax.broadcasted_iota(jnp.int32, (1, rows), 1)
    rel = ((col >> 1) - past_len).astype(F32)
    s = _dot_nt(q, kc) + slope * rel
    s = jnp.where((col & 1) == row_g, s, NEG_INF)
    ncol = lax.broadcasted_iota(jnp.int32, (32, 2 * N_NEW), 1)
    q_t = lax.broadcasted_iota(jnp.int32, (32, 2 * N_NEW), 0) % N_NEW
    s_n = _dot_nt(q, kn_ref[...]) + slope * (ncol >> 1).astype(F32)
    s_n = jnp.where(((ncol & 1) == row_g) & ((ncol >> 1) <= q_t), s_n, NEG_INF)
    m = jnp.maximum(jnp.max(s, axis=-1, keepdims=True), jnp.max(s_n, axis=-1, keepdims=True))
    p = jnp.exp(s - m)
    p_n = jnp.exp(s_n - m)
    l = jnp.sum(p, axis=-1, keepdims=True) + jnp.sum(p_n, axis=-1, keepdims=True)
    o = (_dot(p.astype(BF16), vc) + _dot(p_n.astype(BF16), vn_ref[...])) * (1.0 / l)
    lam = _lambda_value(lp_ref[...], lam_init)
    pd = o[0:16] - lam * o[16:32]
    o_ref[...] = (_rms(pd, subln_ref[...]) * (1.0 - lam_init)).astype(BF16)


def _dec_call(kernel_fn, name, pt, per_seq_inputs, const_inputs, caches, out_rows, out_cols, scratch):
    nseq = pt.shape[0]
    per_seq = lambda a: pl.BlockSpec((None,) + a.shape[1:], lambda n, pt: (n, 0, 0))
    const = lambda a: pl.BlockSpec(a.shape, lambda n, pt: (0, 0))
    grid_spec = pltpu.PrefetchScalarGridSpec(
        num_scalar_prefetch=1,
        grid=(nseq,),
        in_specs=[per_seq(a) if a.ndim == 3 else const(a) for a in per_seq_inputs + const_inputs]
                 + [pl.BlockSpec(memory_space=pl.ANY)] * len(caches),
        out_specs=pl.BlockSpec((None, out_rows, out_cols), lambda n, pt: (n, 0, 0)),
        scratch_shapes=scratch,
    )
    return pl.pallas_call(
        kernel_fn,
        out_shape=jax.ShapeDtypeStruct((nseq, out_rows, out_cols), BF16),
        grid_spec=grid_spec,
        compiler_params=pltpu.CompilerParams(dimension_semantics=("arbitrary",),
                                             vmem_limit_bytes=DEC_VMEM_LIMIT),
        name=name,
    )(pt, *per_seq_inputs, *const_inputs, *caches)


def _dec_a(pt, qa, slopes, kn, vn, lp, subln, cache_k, cache_v, layer, lam_init, past_len):
    rows = 2 * past_len
    scratch = [pltpu.VMEM((2, rows, 128), F32), pltpu.VMEM((2, rows, 128), F32),
               pltpu.SemaphoreType.DMA((2,)), pltpu.SemaphoreType.DMA((2,))]
    kern = functools.partial(_dec_a_kernel, layer=layer, lam_init=lam_init, past_len=past_len)
    return _dec_call(kern, "dec_a", pt, [qa, kn, vn], [slopes, lp, subln], [cache_k, cache_v],
                     16, A_V_DIM, scratch)


def _dec_c_kernel(pt_ref, q_ref, kvn_ref, clat_ref, ckrt_ref, o_ref, latbuf, krbuf, lsem, rsem,
                  *, layer, past_len):
    npages = past_len // PAGE
    slot = _paged_fetch(pt_ref, layer, [(clat_ref, latbuf, lsem, True), (ckrt_ref, krbuf, rsem, False)], npages)
    q = q_ref[...]
    q_lat = q[:, 0:C_KV_RANK]
    q_rope = q[:, C_KV_RANK:C_KV_RANK + C_ROPE_DIM]
    lat = latbuf[slot].astype(BF16)
    krt = krbuf[slot].astype(BF16)
    s = (_dot_nt(q_lat, lat) + _dot(q_rope, krt)) * C_SCALE
    kvn = kvn_ref[...]
    key_t = lax.broadcasted_iota(jnp.int32, (16, 8), 1)
    q_t = lax.broadcasted_iota(jnp.int32, (16, 8), 0) % N_NEW
    s_n = _dot_nt(q, kvn) * C_SCALE
    s_n = jnp.where((key_t <= q_t) & (key_t < N_NEW), s_n, NEG_INF)
    m = jnp.maximum(jnp.max(s, axis=-1, keepdims=True), jnp.max(s_n, axis=-1, keepdims=True))
    p = jnp.exp(s - m)
    p_n = jnp.exp(s_n - m)
    l = jnp.sum(p, axis=-1, keepdims=True) + jnp.sum(p_n, axis=-1, keepdims=True)
    o = _dot(p.astype(BF16), lat) + _dot(p_n.astype(BF16), kvn[:, 0:C_KV_RANK])
    o_ref[...] = (o * (1.0 / l)).astype(BF16)


def _dec_c(pt, qc, kvn, cache_lat, cache_krt, layer, past_len):
    scratch = [pltpu.VMEM((2, past_len, C_KV_RANK), F32), pltpu.VMEM((2, C_ROPE_DIM, past_len), F32),
               pltpu.SemaphoreType.DMA((2,)), pltpu.SemaphoreType.DMA((2,))]
    kern = functools.partial(_dec_c_kernel, layer=layer, past_len=past_len)
    return _dec_call(kern, "dec_c", pt, [qc, kvn], [], [cache_lat, cache_krt], 16, C_KV_RANK, scratch)


def _dec_b_kernel(pt_ref, q_ref, kn_ref, vn_ref, slope_ref, ckt_ref, cvt_ref, o_ref,
                  ktbuf, vtbuf, ksem, vsem, m_sc, l_sc, g_sc, o_sc, *, layer, past_len):
    npages = past_len // PAGE
    nblk = past_len // MOBA_BLOCK
    slot = _paged_fetch(pt_ref, layer, [(ckt_ref, ktbuf, ksem, False), (cvt_ref, vtbuf, vsem, False)], npages)
    q = q_ref[...]
    slope = slope_ref[...][:, 0:1]
    kt = ktbuf[slot].astype(BF16)
    vt = vtbuf[slot].astype(BF16)
    s_raw = _dot(q, kt)
    rel = (lax.broadcasted_iota(jnp.int32, (1, past_len), 1) - past_len).astype(F32)
    s = s_raw + slope * rel
    for b in range(nblk):
        lo = b * MOBA_BLOCK
        sb = s[:, lo:lo + MOBA_BLOCK]
        m_b = jnp.max(sb, axis=-1, keepdims=True)
        p = jnp.exp(sb - m_b)
        l_b = jnp.sum(p, axis=-1, keepdims=True)
        gate = jnp.sum(s_raw[:, lo:lo + MOBA_BLOCK], axis=-1, keepdims=True)
        m_sc[b] = jnp.broadcast_to(m_b, (16, 128))
        l_sc[b] = jnp.broadcast_to(l_b, (16, 128))
        g_sc[b] = jnp.broadcast_to(gate, (16, 128))
        o_sc[b] = _dot_nt(p.astype(BF16), vt[:, lo:lo + MOBA_BLOCK])
    gate = g_sc[...]
    blk = lax.broadcasted_iota(jnp.int32, gate.shape, 0)
    sel = jnp.zeros(gate.shape, F32)
    for _ in range(min(MOBA_TOPK, nblk)):
        mx = jnp.max(gate, axis=0, keepdims=True)
        first = jnp.min(jnp.where(gate == mx, blk, nblk), axis=0, keepdims=True)
        pick = blk == first
        sel = jnp.where(pick, 1.0, sel)
        gate = jnp.where(pick, NEG_INF, gate)
    keep = sel > 0.5
    key_t = lax.broadcasted_iota(jnp.int32, (16, 8), 1)
    q_t = lax.broadcasted_iota(jnp.int32, (16, 8), 0) % N_NEW
    s_n = _dot_nt(q, kn_ref[...]) + slope * key_t.astype(F32)
    s_n = jnp.where((key_t <= q_t) & (key_t < N_NEW), s_n, NEG_INF)
    m_n = jnp.max(s_n, axis=-1, keepdims=True)
    m_all = jnp.where(keep, m_sc[...], NEG_INF)
    m_f = jnp.maximum(jnp.max(m_all, axis=0), m_n)
    w = jnp.where(keep, jnp.exp(m_sc[...] - m_f[None]), 0.0)
    p_n = jnp.exp(s_n - m_f[:, 0:1])
    l_f = jnp.sum(w * l_sc[...], axis=0)[:, 0:1] + jnp.sum(p_n, axis=-1, keepdims=True)
    acc = _dot(p_n.astype(BF16), vn_ref[...])
    acc = acc + jnp.sum(jnp.concatenate([w, w], axis=-1) * o_sc[...], axis=0)
    o = acc * (1.0 / l_f)
    rh = lax.broadcasted_iota(jnp.int32, (16, 256), 0) // N_NEW
    ch = lax.broadcasted_iota(jnp.int32, (16, 256), 1) // B_HEAD_DIM
    o_ref[...] = jnp.where(rh == ch, o, 0.0).astype(BF16)


def _dec_b(pt, qbd, slopes, kn, vn, cache_kt, cache_vt, layer, past_len):
    nblk = past_len // MOBA_BLOCK
    scratch = [pltpu.VMEM((2, 256, past_len), F32), pltpu.VMEM((2, 256, past_len), F32),
               pltpu.SemaphoreType.DMA((2,)), pltpu.SemaphoreType.DMA((2,)),
               pltpu.VMEM((nblk, 16, 128), F32), pltpu.VMEM((nblk, 16, 128), F32),
               pltpu.VMEM((nblk, 16, 128), F32), pltpu.VMEM((nblk, 16, 256), F32)]
    kern = functools.partial(_dec_b_kernel, layer=layer, past_len=past_len)
    return _dec_call(kern, "dec_b", pt, [qbd, kn, vn], [slopes], [cache_kt, cache_vt], 16, 256, scratch)


def _decode_attention(page_table, layer, lam_init, lp, subln, aq, akb, avb, bq, bkb, bvb, qc, ckvb,
                      ca_k, ca_v, cb_kt, cb_vt, cc_lat, cc_krt):
    ns = page_table.shape[0]
    n_new = N_NEW
    ts = ns * n_new
    past_len = page_table.shape[1] * PAGE
    sl_a = np.array([SLOPES_A[2 * g + r] for _ in range(2) for g in range(2) for r in range(2)
                     for _ in range(n_new)], dtype=np.float32)
    sl_b = np.array([SLOPES_B[h] for h in range(B_HEADS) for _ in range(n_new)], dtype=np.float32)
    sl_a = jnp.asarray(np.tile(sl_a[:, None], (1, 128)))
    sl_b = jnp.asarray(np.tile(sl_b[:, None], (1, 128)))
    eye2 = jnp.eye(2, dtype=BF16)
    eye4 = jnp.eye(4, dtype=BF16)

    def pad_new(x):
        x = x.reshape(ns, n_new, x.shape[-1])
        return jnp.pad(x, ((0, 0), (0, 8 - n_new), (0, 0)))

    qa = aq.reshape(ns, n_new, 2, 2, 2, A_QK_DIM).transpose(0, 4, 2, 3, 1, 5)
    qa = jnp.einsum('nmgrtd,mM->nmgrtMd', qa, eye2).reshape(ns, 8 * n_new, 128)
    qb = bq.reshape(ns, n_new, B_HEADS, B_HEAD_DIM).transpose(0, 2, 1, 3)
    qb = jnp.einsum('nhtd,hH->nhtHd', qb, eye4).reshape(ns, 4 * n_new, 256)
    qcs = qc.reshape(ns, n_new, C_HEADS, 256).transpose(0, 2, 1, 3).reshape(ns, 4 * n_new, 256)
    oa = _dec_a(page_table, qa, sl_a, akb.reshape(ns, 2 * n_new, 128), avb.reshape(ns, 2 * n_new, 128),
                lp, subln, ca_k, ca_v, layer, lam_init, past_len)
    ob = _dec_b(page_table, qb, sl_b, pad_new(bkb), pad_new(bvb), cb_kt, cb_vt, layer, past_len)
    oc = _dec_c(page_table, qcs, pad_new(ckvb), cc_lat, cc_krt, layer, past_len)
    oa = oa.reshape(ns, A_HEADS, n_new, A_V_DIM).transpose(0, 2, 1, 3).reshape(ts, 512)
    ob = ob.reshape(ns, B_HEADS, n_new, B_HEADS, B_HEAD_DIM)
    ob = jnp.stack([ob[:, hh, :, hh] for hh in range(B_HEADS)], axis=1)
    ob = ob.transpose(0, 2, 1, 3).reshape(ts, 256)
    oc = oc.reshape(ns, C_HEADS, n_new, C_KV_RANK).transpose(0, 2, 1, 3).reshape(ts, 512)
    return oa, ob, oc


def _rope_tables(pos):
    half = C_ROPE_DIM // 2
    inv = ROPE_BASE ** (-jnp.arange(half, dtype=F32) / half)
    ang = pos.astype(F32)[:, None] * inv[None, :]
    cos, sin = jnp.cos(ang), jnp.sin(ang)
    qcs = jnp.tile(cos, (1, 8))
    qsn = jnp.concatenate([jnp.tile(-sin, (1, 4)), jnp.tile(sin, (1, 4))], axis=1)
    pad = jnp.zeros((pos.shape[0], 128 - C_ROPE_DIM), F32)
    kcs = jnp.concatenate([cos, cos, pad], axis=1)
    ksn = jnp.concatenate([-sin, sin, pad], axis=1)
    return qcs, qsn, kcs, ksn


def _uq_perm():
    per = C_NOPE_DIM + C_ROPE_DIM
    half = C_ROPE_DIM // 2
    nope = [h * per + d for h in range(C_HEADS) for d in range(C_NOPE_DIM)]
    r1 = [h * per + C_NOPE_DIM + j for h in range(C_HEADS) for j in range(half)]
    r2 = [h * per + C_NOPE_DIM + half + j for h in range(C_HEADS) for j in range(half)]
    return np.array(nope + r1 + r2, dtype=np.int32)


def _rope_placement():
    half = C_ROPE_DIM // 2
    pm = np.zeros((128, C_HEADS * 256), dtype=np.float32)
    for h in range(C_HEADS):
        for j in range(half):
            pm[h * half + j, h * 256 + C_KV_RANK + j] = 1.0
            pm[64 + h * half + j, h * 256 + C_KV_RANK + half + j] = 1.0
    return pm


def _layer_weights(l, norm_gains, ffn_w_gate_up, ffn_w_down, w_in, w_out, diff_subln,
                   mla_q_norm, mla_w_uq, mla_kv_norm, mla_w_ukv):
    eye = jnp.eye(C_HEADS, dtype=F32)
    w_ukv = mla_w_ukv[l].reshape(C_KV_RANK, C_HEADS, C_NOPE_DIM + C_V_DIM)
    w_uk = w_ukv[..., :C_NOPE_DIM]
    w_uv = w_ukv[..., C_NOPE_DIM:]
    uk_bd = jnp.einsum('chd,hg->hdgc', w_uk, eye)
    uk_bd = jnp.pad(uk_bd, ((0, 0), (0, 0), (0, 0), (0, 256 - C_KV_RANK))).reshape(256, C_HEADS * 256)
    wqc = jnp.concatenate([uk_bd, jnp.asarray(_rope_placement())], axis=0).astype(BF16)
    wuv = jnp.einsum('chv,hg->hcgv', w_uv, eye).reshape(C_HEADS * C_KV_RANK, C_HEADS * C_V_DIM).astype(BF16)
    g = norm_gains[l]
    return dict(
        gains=[g[i][None, :] for i in range(6)],
        wgu=ffn_w_gate_up[l].astype(BF16), wdn=ffn_w_down[l].astype(BF16),
        win=jnp.pad(w_in[l], ((0, 0), (0, IN_PAD - IN_COLS))).astype(BF16),
        wo=w_out[l].astype(BF16),
        subln=diff_subln[l][None, :],
        qn=mla_q_norm[l][None, :], kvn=mla_kv_norm[l][None, :],
        wuq=mla_w_uq[l][:, _uq_perm()].astype(BF16), wqc=wqc, wuv=wuv)


def kernel(x_prompt, x_sample, cache_a_k, cache_a_v, cache_b_k, cache_b_v, cache_c_latent, cache_c_krope,
           page_table, norm_gains, ffn_w_gate_up, ffn_w_down, w_in, w_out, diff_lambda, diff_subln,
           mla_q_norm, mla_w_uq, mla_kv_norm, mla_w_ukv):
    nb, seq, _ = x_prompt.shape
    ns, n_new, _ = x_sample.shape
    depth = w_in.shape[0]
    n_pool, page = cache_a_k.shape[1], cache_a_k.shape[2]
    past_len = page_table.shape[1] * page
    tp, ts = nb * seq, ns * n_new
    assert seq % TQ == 0 and TQ == MOBA_BLOCK and page == PAGE and past_len % MOBA_BLOCK == 0
    assert n_new == N_NEW

    tabs_p = _rope_tables(jnp.arange(seq, dtype=jnp.int32))
    tabs_s = _rope_tables(jnp.tile(past_len + jnp.arange(n_new, dtype=jnp.int32), ns))
    ca_k = cache_a_k.reshape(depth, n_pool, 2 * page, 128)
    ca_v = cache_a_v.reshape(depth, n_pool, 2 * page, 128)
    cb_kt = cache_b_k.transpose(0, 1, 3, 4, 2).reshape(depth, n_pool, 256, page)
    cb_vt = cache_b_v.transpose(0, 1, 3, 4, 2).reshape(depth, n_pool, 256, page)
    cc_krt = cache_c_krope.transpose(0, 1, 3, 2)

    h_p = x_prompt.reshape(tp, D_MODEL)
    h_s = x_sample.reshape(ts, D_MODEL)
    rows_p, rows_s = [], []
    for l in range(depth):
        lam_init = 0.8 - 0.6 * math.exp(-0.3 * l)
        w = _layer_weights(l, norm_gains, ffn_w_gate_up, ffn_w_down, w_in, w_out, diff_subln,
                           mla_q_norm, mla_w_uq, mla_kv_norm, mla_w_ukv)
        gn = w['gains']
        lp = diff_lambda[l]

        h1 = _ffn(h_p, gn[0], gn[1], w['wgu'][0], w['wdn'][0], 512)
        (aq, ak, av, bq, bqf, bk, bv, qc, clat, ckr, akb, avb, bkb, bvb, ckvb) = _proj(
            h1, gn[2], w['win'], w['qn'], w['wuq'], w['wqc'], w['kvn'], tabs_p, 512, seq // 512)
        oa = _attn_a(aq, akb, avb, lp, w['subln'], lam_init, nb, seq)
        ob = _attn_b(bq, bqf, bkb, bk, bvb, nb, seq)
        oc = _attn_c(qc, ckvb, nb, seq)
        h2 = _merge(h1, oa, ob, oc, w['wuv'], w['wo'], gn[3], 512)
        h_p = _ffn(h2, gn[4], gn[5], w['wgu'][1], w['wdn'][1], 512)
        rows_p.append((ak.reshape(nb, seq, A_KV_HEADS, 2 * A_QK_DIM), av.reshape(nb, seq, A_KV_HEADS, A_V_DIM),
                       bk.reshape(nb, seq, B_HEADS, B_HEAD_DIM), bv.reshape(nb, seq, B_HEADS, B_HEAD_DIM),
                       clat.reshape(nb, seq, C_KV_RANK), ckr.reshape(nb, seq, C_ROPE_DIM)))

        g1 = _ffn(h_s, gn[0], gn[1], w['wgu'][0], w['wdn'][0], ts)
        (aq, ak, av, bq, bqf, bk, bv, qc, clat, ckr, akb, avb, bkb, bvb, ckvb) = _proj(
            g1, gn[2], w['win'], w['qn'], w['wuq'], w['wqc'], w['kvn'], tabs_s, ts, 1)
        oa, ob, oc = _decode_attention(page_table, l, lam_init, lp, w['subln'], aq, akb, avb, bq, bkb, bvb,
                                       qc, ckvb, ca_k, ca_v, cb_kt, cb_vt, cache_c_latent, cc_krt)
        g2 = _merge(g1, oa, ob, oc, w['wuv'], w['wo'], gn[3], ts)
        h_s = _ffn(g2, gn[4], gn[5], w['wgu'][1], w['wdn'][1], ts)
        rows_s.append((ak.reshape(ns, n_new, A_KV_HEADS, 2 * A_QK_DIM), av.reshape(ns, n_new, A_KV_HEADS, A_V_DIM),
                       bk.reshape(ns, n_new, B_HEADS, B_HEAD_DIM), bv.reshape(ns, n_new, B_HEADS, B_HEAD_DIM),
                       clat.reshape(ns, n_new, C_KV_RANK), ckr.reshape(ns, n_new, C_ROPE_DIM)))

    stack = lambda rows, i: jnp.stack([r[i] for r in rows])
    return (h_p.reshape(nb, seq, D_MODEL), h_s.reshape(ns, n_new, D_MODEL),
            *[stack(rows_p, i) for i in range(6)], *[stack(rows_s, i) for i in range(6)])
```

```python
import functools
import math

import numpy as np
import jax
import jax.numpy as jnp
from jax import lax
from jax.experimental import pallas as pl
from jax.experimental.pallas import tpu as pltpu

F32 = jnp.float32
BF16 = jnp.bfloat16

D_MODEL = 1024
D_FF = 2816
A_HEADS = 4
A_KV_HEADS = 2
A_QK_DIM = 64
A_V_DIM = 128
B_HEADS = 4
B_HEAD_DIM = 64
MOBA_BLOCK = 256
MOBA_TOPK = 3
C_HEADS = 4
C_Q_RANK = 256
C_KV_RANK = 128
C_NOPE_DIM = 64
C_ROPE_DIM = 32
C_V_DIM = 64
C_SCALE = (C_NOPE_DIM + C_ROPE_DIM) ** -0.5
ROPE_BASE = 10000.0
NORM_EPS = 1e-6
QK_SCALE = 0.125

IN_COLS = 2208
IN_PAD = 2304
TQ = 256
PAGE = 128
NEG_INF = float("-inf")

VMEM_LIMIT = 56 * 1024 * 1024

SLOPES_A = tuple(2.0 ** -(2 * h + 1) for h in range(A_HEADS))
SLOPES_B = tuple(2.0 ** -(2 * h + 2) for h in range(B_HEADS))


def _rms(x, g):
    return x * lax.rsqrt(jnp.mean(x * x, axis=-1, keepdims=True) + NORM_EPS) * g


def _dot(a, b):
    return jnp.dot(a, b, preferred_element_type=F32)


def _dot_nt(a, b, precision=None):
    return lax.dot_general(a, b, (((1,), (1,)), ((), ())), preferred_element_type=F32,
                           precision=precision)


def _const_spec(shape):
    nd = len(shape)
    return pl.BlockSpec(shape, lambda *_: (0,) * nd, pipeline_mode=pl.Buffered(1))


def _stacked_spec(shape, lead):
    nd = len(shape)
    return pl.BlockSpec((None,) * len(lead) + tuple(shape), lambda *_: tuple(lead) + (0,) * nd,
                        pipeline_mode=pl.Buffered(1))


def _params(sem):
    return pltpu.CompilerParams(dimension_semantics=sem, vmem_limit_bytes=VMEM_LIMIT)


FFN_CHUNKS = 2
FFN_CHUNK = D_FF // FFN_CHUNKS


def _ffn_body(x, gpre, gpost, wgu_ref, wdn_ref):
    xn = _rms(x, gpre).astype(BF16)
    acc = None
    for c in range(FFN_CHUNKS):
        lo = c * FFN_CHUNK
        g = _dot(xn, wgu_ref[:, lo:lo + FFN_CHUNK])
        u = _dot(xn, wgu_ref[:, D_FF + lo:D_FF + lo + FFN_CHUNK])
        a = (g * (1.0 / (1.0 + jnp.exp(-g))) * u).astype(BF16)
        part = _dot(a, wdn_ref[lo:lo + FFN_CHUNK, :])
        acc = part if acc is None else acc + part
    return x + 0.5 * _rms(acc, gpost)


def _ffn_kernel(x_ref, gpre_ref, gpost_ref, wgu_ref, wdn_ref, o_ref):
    o_ref[...] = _ffn_body(x_ref[...], gpre_ref[...], gpost_ref[...], wgu_ref, wdn_ref)


def _ffn(x, gpre, gpost, wgu, wdn, lead, tm):
    t = x.shape[0]
    return pl.pallas_call(
        _ffn_kernel,
        out_shape=jax.ShapeDtypeStruct((t, D_MODEL), F32),
        grid=(t // tm,),
        in_specs=[pl.BlockSpec((tm, D_MODEL), lambda i: (i, 0)),
                  _const_spec((1, D_MODEL)), _const_spec((1, D_MODEL)),
                  _stacked_spec((D_MODEL, 2 * D_FF), lead), _stacked_spec((D_FF, D_MODEL), lead)],
        out_specs=pl.BlockSpec((tm, D_MODEL), lambda i: (i, 0)),
        compiler_params=_params(("parallel",)),
        name="ffn",
    )(x, gpre, gpost, wgu, wdn)


N_PROJ_IN = 11
N_WORK_OUT = 9
N_ROW_OUT = 6


def _proj_kernel(*refs, n_prev):
    (h_ref, g_ref, win_ref, qn_ref, wuq_ref, wqc_ref, kvn_ref,
     qcs_ref, qsn_ref, kcs_ref, ksn_ref) = refs[:N_PROJ_IN]
    prev_refs = refs[N_PROJ_IN:N_PROJ_IN + n_prev]
    (aq_ref, bq_ref, bqf_ref, qc_ref, akb_ref, avb_ref, bkb_ref, bvb_ref, ckvb_ref) = \
        refs[N_PROJ_IN + n_prev:N_PROJ_IN + n_prev + N_WORK_OUT]
    row_refs = refs[N_PROJ_IN + n_prev + N_WORK_OUT:]
    if n_prev:
        for prev_ref, row_ref in zip(prev_refs, row_refs):
            row_ref[0] = prev_ref[...]
        row_refs = [r.at[1] for r in row_refs]
    ak_ref, av_ref, bk_ref, bv_ref, clat_ref, ckr_ref = row_refs
    tm = h_ref.shape[0]
    u = _rms(h_ref[...], g_ref[...]).astype(BF16)
    z = _dot(u, win_ref[...])
    aq_ref[...] = (z[:, 0:512] * QK_SCALE).astype(BF16)
    ak = z[:, 512:768]
    akb_ref[...] = ak.astype(BF16)
    av = z[:, 768:1024]
    avb_ref[...] = av.astype(BF16)
    for g in range(A_KV_HEADS):
        ak_ref[pl.ds(g, tm, stride=A_KV_HEADS), :] = ak[:, g * 128:(g + 1) * 128]
        av_ref[pl.ds(g, tm, stride=A_KV_HEADS), :] = av[:, g * 128:(g + 1) * 128]
    bq = z[:, 1024:1280]
    bqf_ref[...] = bq
    bq_ref[...] = (bq * QK_SCALE).astype(BF16)
    bk = z[:, 1280:1536]
    bk_ref[...] = bk
    bkb_ref[...] = bk.astype(BF16)
    bv = z[:, 1536:1792]
    bv_ref[...] = bv
    bvb_ref[...] = bv.astype(BF16)
    cqn = _rms(z[:, 1792:2048], qn_ref[...]).astype(BF16)
    q = _dot(cqn, wuq_ref[...])
    qr = q[:, 256:384]
    qrot = qr * qcs_ref[...] + pltpu.roll(qr, 64, 1) * qsn_ref[...]
    qcat = jnp.concatenate([q[:, 0:256], qrot], axis=1).astype(BF16)
    qc_ref[...] = _dot(qcat, wqc_ref[...]).astype(BF16)
    clat = _rms(z[:, 2048:2176], kvn_ref[...])
    clat_ref[...] = clat
    kr = z[:, 2176:2304]
    lane = lax.broadcasted_iota(jnp.int32, kr.shape, 1)
    swapped = jnp.where(lane < 16, pltpu.roll(kr, 112, 1), pltpu.roll(kr, 16, 1))
    krot = kr * kcs_ref[...] + swapped * ksn_ref[...]
    ckr_ref[...] = krot[:, 0:C_ROPE_DIM]
    ckvb_ref[...] = jnp.concatenate([clat, krot], axis=1).astype(BF16)


def _proj(h, g, win, layer, qn, wuq, wqc, kvn, tabs, tm, tab_blocks, prev_rows):
    t = h.shape[0]
    row = lambda w: pl.BlockSpec((tm, w), lambda i: (i, 0))
    tab = pl.BlockSpec((tm, 128), lambda i: (i % tab_blocks, 0))
    work = [(512, BF16), (256, BF16), (256, F32), (1024, BF16),
            (256, BF16), (256, BF16), (256, BF16), (256, BF16), (256, BF16)]
    assert len(work) == N_WORK_OUT
    rows = [(A_KV_HEADS, 128), (A_KV_HEADS, 128), (1, 256), (1, 256), (1, C_KV_RANK), (1, C_ROPE_DIM)]
    if prev_rows is None:
        n_prev = 0
        row_shapes = [jax.ShapeDtypeStruct((r * t, w), F32) for r, w in rows]
        row_specs = [pl.BlockSpec((r * tm, w), lambda i: (i, 0)) for r, w in rows]
        prev_specs = []
    else:
        n_prev = N_ROW_OUT
        row_shapes = [jax.ShapeDtypeStruct((2, r * t, w), F32) for r, w in rows]
        row_specs = [pl.BlockSpec((2, r * tm, w), lambda i: (0, i, 0)) for r, w in rows]
        prev_specs = [pl.BlockSpec((r * tm, w), lambda i: (i, 0)) for r, w in rows]
    out = pl.pallas_call(
        functools.partial(_proj_kernel, n_prev=n_prev),
        out_shape=[jax.ShapeDtypeStruct((t, w), d) for w, d in work] + row_shapes,
        grid=(t // tm,),
        in_specs=[row(D_MODEL), _const_spec((1, D_MODEL)), _stacked_spec((D_MODEL, IN_PAD), (layer,)),
                  _const_spec((1, C_Q_RANK)), _const_spec((C_Q_RANK, 384)), _const_spec((384, 1024)),
                  _const_spec((1, C_KV_RANK)), tab, tab, tab, tab] + prev_specs,
        out_specs=[row(w) for w, _ in work] + row_specs,
        compiler_params=_params(("parallel",)),
        name="proj",
    )(h, g, win, qn, wuq, wqc, kvn, *tabs, *([] if prev_rows is None else prev_rows))
    return out[:N_WORK_OUT], out[N_WORK_OUT:]


def _lambda_value(lp, lam_init):
    return (jnp.exp(jnp.sum(lp[0:1, :] * lp[1:2, :], axis=-1, keepdims=True))
            - jnp.exp(jnp.sum(lp[2:3, :] * lp[3:4, :], axis=-1, keepdims=True)) + lam_init)


def _causal_tile(reps):
    row = lax.broadcasted_iota(jnp.int32, (TQ, TQ), 0)
    col = lax.broadcasted_iota(jnp.int32, (TQ, TQ), 1)
    m = col <= row
    return jnp.concatenate([m] * reps, axis=0)


def _softmax_parts(s_p, s_d):
    m = jnp.max(s_d, axis=-1, keepdims=True)
    if s_p is not None:
        m = jnp.maximum(m, jnp.max(s_p, axis=-1, keepdims=True))
    p_d = jnp.exp(s_d - m)
    l = jnp.sum(p_d, axis=-1, keepdims=True)
    p_p = None
    if s_p is not None:
        p_p = jnp.exp(s_p - m)
        l = l + jnp.sum(p_p, axis=-1, keepdims=True)
    return p_p, p_d, l


def _attn_a_kernel(q_ref, k_ref, kpos_ref, v_ref, lp_ref, subln_ref, o_ref, *, lam_init, seq):
    g = pl.program_id(1)
    lam = _lambda_value(lp_ref[...], lam_init)
    lane = lax.broadcasted_iota(jnp.int32, (TQ, 128), 1)
    is_m0 = lane < A_QK_DIM
    causal = _causal_tile(4)
    rowblk = lax.broadcasted_iota(jnp.int32, (4 * TQ, 1), 0) // TQ
    s_r0 = jnp.where(g == 0, SLOPES_A[0], SLOPES_A[2])
    s_r1 = jnp.where(g == 0, SLOPES_A[1], SLOPES_A[3])
    slope = jnp.where(rowblk % 2 == 0, s_r0, s_r1).astype(F32)
    coef_sign = jnp.where(rowblk < 2, 1.0, -lam)
    q_pos = jnp.where(lax.broadcasted_iota(jnp.int32, (4 * TQ, 128), 1) < 2, slope, 0.0).astype(BF16)
    kk = jnp.concatenate([k_ref[...], kpos_ref[...]], axis=1)
    for qi in range(seq // TQ):
        r0 = qi * TQ
        q0 = q_ref[r0:r0 + TQ, 0:128]
        q1 = q_ref[r0:r0 + TQ, 128:256]
        zero = jnp.zeros_like(q0)
        qz = jnp.concatenate([jnp.where(is_m0, q0, zero), jnp.where(is_m0, q1, zero),
                              jnp.where(is_m0, zero, q0), jnp.where(is_m0, zero, q1)], axis=0)
        qz = jnp.concatenate([qz, q_pos], axis=1)
        s_d = jnp.where(causal, _dot_nt(qz, kk[r0:r0 + TQ, :]), NEG_INF)
        s_p = _dot_nt(qz, kk[0:r0, :]) if qi > 0 else None
        p_p, p_d, l = _softmax_parts(s_p, s_d)
        coef = coef_sign / l
        w_d = p_d * coef
        pd_d = (w_d[0:2 * TQ] + w_d[2 * TQ:4 * TQ]).astype(BF16)
        o = _dot(pd_d, v_ref[r0:r0 + TQ, :])
        if qi > 0:
            w_p = p_p * coef
            pd_p = (w_p[0:2 * TQ] + w_p[2 * TQ:4 * TQ]).astype(BF16)
            o = o + _dot(pd_p, v_ref[0:r0, :])
        y = _rms(o, subln_ref[...]) * (1.0 - lam_init)
        o_ref[r0:r0 + TQ, 0:128] = y[0:TQ].astype(BF16)
        o_ref[r0:r0 + TQ, 128:256] = y[TQ:2 * TQ].astype(BF16)


def _key_position_columns(seq):
    pos = np.arange(seq)
    cols = np.zeros((seq, 128), dtype=np.float32)
    cols[:, 0] = (pos // 256) * 256
    cols[:, 1] = pos % 256
    return jnp.asarray(cols).astype(BF16)


def _attn_a(aq, akb, avb, lp, subln, lam_init, nb, seq):
    return pl.pallas_call(
        functools.partial(_attn_a_kernel, lam_init=lam_init, seq=seq),
        out_shape=jax.ShapeDtypeStruct((nb * seq, 512), BF16),
        grid=(nb, A_KV_HEADS),
        in_specs=[pl.BlockSpec((seq, 256), lambda n, g: (n, g)),
                  pl.BlockSpec((seq, 128), lambda n, g: (n, g)),
                  _const_spec((seq, 128)),
                  pl.BlockSpec((seq, 128), lambda n, g: (n, g)),
                  _const_spec((4, A_QK_DIM)), _const_spec((1, A_V_DIM))],
        out_specs=pl.BlockSpec((seq, 256), lambda n, g: (n, g)),
        compiler_params=_params(("parallel", "parallel")),
        name="attn_a",
    )(aq, akb, _key_position_columns(seq), avb, lp, subln)


def _attn_c_kernel(q_ref, kv_ref, o_ref, *, seq):
    causal = _causal_tile(C_HEADS)
    for qi in range(seq // TQ):
        r0 = qi * TQ
        q4 = jnp.concatenate([q_ref[r0:r0 + TQ, h * 256:(h + 1) * 256] for h in range(C_HEADS)], axis=0)
        s_d = jnp.where(causal, _dot_nt(q4, kv_ref[r0:r0 + TQ, :]) * C_SCALE, NEG_INF)
        s_p = _dot_nt(q4, kv_ref[0:r0, :]) * C_SCALE if qi > 0 else None
        p_p, p_d, l = _softmax_parts(s_p, s_d)
        o = _dot(p_d.astype(BF16), kv_ref[r0:r0 + TQ, 0:C_KV_RANK])
        if qi > 0:
            o = o + _dot(p_p.astype(BF16), kv_ref[0:r0, 0:C_KV_RANK])
        o = o * (1.0 / l)
        for h in range(C_HEADS):
            o_ref[r0:r0 + TQ, h * 128:(h + 1) * 128] = o[h * TQ:(h + 1) * TQ].astype(BF16)


def _attn_c(qc, ckvb, nb, seq):
    return pl.pallas_call(
        functools.partial(_attn_c_kernel, seq=seq),
        out_shape=jax.ShapeDtypeStruct((nb * seq, C_HEADS * C_KV_RANK), BF16),
        grid=(nb,),
        in_specs=[pl.BlockSpec((seq, 1024), lambda n: (n, 0)),
                  pl.BlockSpec((seq, 256), lambda n: (n, 0))],
        out_specs=pl.BlockSpec((seq, C_HEADS * C_KV_RANK), lambda n: (n, 0)),
        compiler_params=_params(("parallel",)),
        name="attn_c",
    )(qc, ckvb)


def _head_masks(rows):
    lane = lax.broadcasted_iota(jnp.int32, (rows, B_HEADS * B_HEAD_DIM), 1)
    return [(lane >= h * B_HEAD_DIM) & (lane < (h + 1) * B_HEAD_DIM) for h in range(B_HEADS)]


def _attn_b_kernel(q_ref, qf_ref, k_ref, kf_ref, v_ref, o_ref, *, seq):
    nblk = seq // MOBA_BLOCK
    hm = _head_masks(TQ)
    causal = _causal_tile(B_HEADS)
    rowblk = lax.broadcasted_iota(jnp.int32, (B_HEADS * TQ, 1), 0) // TQ
    slope = jnp.where(rowblk == 0, SLOPES_B[0],
                      jnp.where(rowblk == 1, SLOPES_B[1],
                                jnp.where(rowblk == 2, SLOPES_B[2], SLOPES_B[3]))).astype(F32)
    rel_d = lax.broadcasted_iota(jnp.int32, (1, TQ), 1).astype(F32)
    kmean = jnp.concatenate(
        [jnp.sum(kf_ref[b * MOBA_BLOCK:(b + 1) * MOBA_BLOCK, :], axis=0, keepdims=True) for b in range(nblk)],
        axis=0) * (1.0 / MOBA_BLOCK)
    blk = lax.broadcasted_iota(jnp.int32, (B_HEADS * TQ, nblk), 1)
    for qi in range(nblk):
        r0 = qi * TQ
        q = q_ref[r0:r0 + TQ, :]
        qz = jnp.concatenate([jnp.where(hm[h], q, jnp.zeros_like(q)) for h in range(B_HEADS)], axis=0)
        s_d = _dot_nt(qz, k_ref[r0:r0 + TQ, :]) + slope * rel_d
        s_d = jnp.where(causal, s_d, NEG_INF)
        s_p = None
        if qi > 0:
            rel_p = (lax.broadcasted_iota(jnp.int32, (1, r0), 1) - r0).astype(F32)
            s_p = _dot_nt(qz, k_ref[0:r0, :]) + slope * rel_p
            if qi > MOBA_TOPK:
                qf = qf_ref[r0:r0 + TQ, :]
                qfz = jnp.concatenate([jnp.where(hm[h], qf, jnp.zeros_like(qf)) for h in range(B_HEADS)],
                                      axis=0)
                gate = _dot_nt(qfz, kmean, precision=lax.Precision.HIGHEST)
                gate = jnp.where(blk < qi, gate, NEG_INF)
                picks = []
                for _ in range(MOBA_TOPK):
                    mx = jnp.max(gate, axis=-1, keepdims=True)
                    first = jnp.min(jnp.where(gate == mx, blk, nblk), axis=-1, keepdims=True)
                    picks.append(first)
                    gate = jnp.where(blk == first, NEG_INF, gate)
                pieces = []
                for b in range(qi):
                    keep = (picks[0] == b) | (picks[1] == b) | (picks[2] == b)
                    pieces.append(jnp.where(keep, s_p[:, b * MOBA_BLOCK:(b + 1) * MOBA_BLOCK], NEG_INF))
                s_p = jnp.concatenate(pieces, axis=1)
        p_p, p_d, l = _softmax_parts(s_p, s_d)
        o = _dot(p_d.astype(BF16), v_ref[r0:r0 + TQ, :])
        if qi > 0:
            o = o + _dot(p_p.astype(BF16), v_ref[0:r0, :])
        o = o * (1.0 / l)
        out = jnp.where(hm[0], o[0:TQ], 0.0)
        for h in range(1, B_HEADS):
            out = out + jnp.where(hm[h], o[h * TQ:(h + 1) * TQ], 0.0)
        o_ref[r0:r0 + TQ, :] = out.astype(BF16)


def _attn_b(bq, bqf, bkb, bk, bvb, nb, seq):
    spec = pl.BlockSpec((seq, 256), lambda n: (n, 0))
    bk_spec = spec if bk.ndim == 2 else pl.BlockSpec((None, seq, 256), lambda n: (bk.shape[0] - 1, n, 0))
    return pl.pallas_call(
        functools.partial(_attn_b_kernel, seq=seq),
        out_shape=jax.ShapeDtypeStruct((nb * seq, 256), BF16),
        grid=(nb,),
        in_specs=[spec, spec, spec, bk_spec, spec],
        out_specs=spec,
        compiler_params=_params(("parallel",)),
        name="attn_b",
    )(bq, bqf, bkb, bk, bvb)


def _merge_kernel(h_ref, oa_ref, ob_ref, oc_ref, wuv_ref, wo_ref, g_ref, o_ref):
    oc = _dot(oc_ref[...], wuv_ref[...]).astype(BF16)
    mix = (_dot(oa_ref[...], wo_ref[0:512, :]) + _dot(ob_ref[...], wo_ref[512:768, :])
           + _dot(oc, wo_ref[768:1024, :]))
    o_ref[...] = h_ref[...] + _rms(mix, g_ref[...])


def _merge(h, oa, ob, oc, wuv, wo, layer, g, tm):
    t = h.shape[0]
    row = lambda w: pl.BlockSpec((tm, w), lambda i: (i, 0))
    return pl.pallas_call(
        _merge_kernel,
        out_shape=jax.ShapeDtypeStruct((t, D_MODEL), F32),
        grid=(t // tm,),
        in_specs=[row(D_MODEL), row(512), row(256), row(512),
                  _const_spec((512, 256)), _stacked_spec((D_MODEL, D_MODEL), (layer,)),
                  _const_spec((1, D_MODEL))],
        out_specs=row(D_MODEL),
        compiler_params=_params(("parallel",)),
        name="merge",
    )(h, oa, ob, oc, wuv, wo, g)


N_NEW = 4
DEC_VMEM_LIMIT = 58 * 1024 * 1024
ISSUE_UNROLL = 8


def _page_copy(pt_ref, seq, j, layer, cache_ref, buf_ref, slot, sem, stack_rows):
    src = cache_ref.at[layer, pt_ref[seq, j]]
    if stack_rows:
        r = cache_ref.shape[2]
        dst = buf_ref.at[slot, pl.ds(pl.multiple_of(j * r, r), r), :]
    else:
        dst = buf_ref.at[slot, :, pl.ds(pl.multiple_of(j * PAGE, PAGE), PAGE)]
    return pltpu.make_async_copy(src, dst, sem)


def _paged_fetch(pt_ref, layer, streams, npages):
    n = pl.program_id(0)
    slot = n % 2

    def for_all_pages(seq, sl, act):
        def body(j, carry):
            for cache_ref, buf_ref, sem_ref, stack_rows in streams:
                act(_page_copy(pt_ref, seq, j, layer, cache_ref, buf_ref, sl, sem_ref.at[sl], stack_rows))
            return carry
        lax.fori_loop(0, npages, body, 0, unroll=ISSUE_UNROLL)

    @pl.when(n == 0)
    def _():
        for_all_pages(0, 0, lambda cp: cp.start())

    @pl.when(n + 1 < pl.num_programs(0))
    def _():
        for_all_pages(n + 1, 1 - slot, lambda cp: cp.start())

    for_all_pages(n, slot, lambda cp: cp.wait())
    return slot


def _dec_a_kernel(pt_ref, q_ref, kn_ref, vn_ref, slope_ref, lp_ref, subln_ref, ck_ref, cv_ref,
                  o_ref, kbuf, vbuf, ksem, vsem, *, layer, lam_init, past_len):
    npages = past_len // PAGE
    slot = _paged_fetch(pt_ref, layer, [(ck_ref, kbuf, ksem, True), (cv_ref, vbuf, vsem, True)], npages)
    rows = 2 * past_len
    q = q_ref[...]
    slope = slope_ref[...][:, 0:1]
    row_g = (lax.broadcasted_iota(jnp.int32, (32, 1), 0) // (2 * N_NEW)) % 2
    kc = kbuf[slot].astype(BF16)
    vc = vbuf[slot].astype(BF16)
    col = lax.broadcasted_iota(jnp.int32, (1, rows), 1)
    rel = ((col >> 1) - past_len).astype(F32)
    s = _dot_nt(q, kc) + slope * rel
    s = jnp.where((col & 1) == row_g, s, NEG_INF)
    ncol = lax.broadcasted_iota(jnp.int32, (32, 2 * N_NEW), 1)
    q_t = lax.broadcasted_iota(jnp.int32, (32, 2 * N_NEW), 0) % N_NEW
    s_n = _dot_nt(q, kn_ref[...]) + slope * (ncol >> 1).astype(F32)
    s_n = jnp.where(((ncol & 1) == row_g) & ((ncol >> 1) <= q_t), s_n, NEG_INF)
    m = jnp.maximum(jnp.max(s, axis=-1, keepdims=True), jnp.max(s_n, axis=-1, keepdims=True))
    p = jnp.exp(s - m)
    p_n = jnp.exp(s_n - m)
    l = jnp.sum(p, axis=-1, keepdims=True) + jnp.sum(p_n, axis=-1, keepdims=True)
    o = (_dot(p.astype(BF16), vc) + _dot(p_n.astype(BF16), vn_ref[...])) * (1.0 / l)
    lam = _lambda_value(lp_ref[...], lam_init)
    pd = o[0:16] - lam * o[16:32]
    o_ref[...] = (_rms(pd, subln_ref[...]) * (1.0 - lam_init)).astype(BF16)


def _dec_call(kernel_fn, name, pt, per_seq_inputs, const_inputs, caches, out_rows, out_cols, scratch):
    nseq = pt.shape[0]
    per_seq = lambda a: pl.BlockSpec((None,) + a.shape[1:], lambda n, pt: (n, 0, 0))
    const = lambda a: pl.BlockSpec(a.shape, lambda n, pt: (0, 0))
    grid_spec = pltpu.PrefetchScalarGridSpec(
        num_scalar_prefetch=1,
        grid=(nseq,),
        in_specs=[per_seq(a) if a.ndim == 3 else const(a) for a in per_seq_inputs + const_inputs]
                 + [pl.BlockSpec(memory_space=pl.ANY)] * len(caches),
        out_specs=pl.BlockSpec((None, out_rows, out_cols), lambda n, pt: (n, 0, 0)),
        scratch_shapes=scratch,
    )
    return pl.pallas_call(
        kernel_fn,
        out_shape=jax.ShapeDtypeStruct((nseq, out_rows, out_cols), BF16),
        grid_spec=grid_spec,
        compiler_params=pltpu.CompilerParams(dimension_semantics=("arbitrary",),
                                             vmem_limit_bytes=DEC_VMEM_LIMIT),
        name=name,
    )(pt, *per_seq_inputs, *const_inputs, *caches)


def _dec_a(pt, qa, slopes, kn, vn, lp, subln, cache_k, cache_v, layer, lam_init, past_len):
    rows = 2 * past_len
    scratch = [pltpu.VMEM((2, rows, 128), F32), pltpu.VMEM((2, rows, 128), F32),
               pltpu.SemaphoreType.DMA((2,)), pltpu.SemaphoreType.DMA((2,))]
    kern = functools.partial(_dec_a_kernel, layer=layer, lam_init=lam_init, past_len=past_len)
    return _dec_call(kern, "dec_a", pt, [qa, kn, vn], [slopes, lp, subln], [cache_k, cache_v],
                     16, A_V_DIM, scratch)


def _dec_c_kernel(pt_ref, q_ref, kvn_ref, clat_ref, ckrt_ref, o_ref, latbuf, krbuf, lsem, rsem,
                  *, layer, past_len):
    npages = past_len // PAGE
    slot = _paged_fetch(pt_ref, layer, [(clat_ref, latbuf, lsem, True), (ckrt_ref, krbuf, rsem, False)], npages)
    q = q_ref[...]
    q_lat = q[:, 0:C_KV_RANK]
    q_rope = q[:, C_KV_RANK:C_KV_RANK + C_ROPE_DIM]
    lat = latbuf[slot].astype(BF16)
    krt = krbuf[slot].astype(BF16)
    s = (_dot_nt(q_lat, lat) + _dot(q_rope, krt)) * C_SCALE
    kvn = kvn_ref[...]
    key_t = lax.broadcasted_iota(jnp.int32, (16, 8), 1)
    q_t = lax.broadcasted_iota(jnp.int32, (16, 8), 0) % N_NEW
    s_n = _dot_nt(q, kvn) * C_SCALE
    s_n = jnp.where((key_t <= q_t) & (key_t < N_NEW), s_n, NEG_INF)
    m = jnp.maximum(jnp.max(s, axis=-1, keepdims=True), jnp.max(s_n, axis=-1, keepdims=True))
    p = jnp.exp(s - m)
    p_n = jnp.exp(s_n - m)
    l = jnp.sum(p, axis=-1, keepdims=True) + jnp.sum(p_n, axis=-1, keepdims=True)
    o = _dot(p.astype(BF16), lat) + _dot(p_n.astype(BF16), kvn[:, 0:C_KV_RANK])
    o_ref[...] = (o * (1.0 / l)).astype(BF16)


def _dec_c(pt, qc, kvn, cache_lat, cache_krt, layer, past_len):
    scratch = [pltpu.VMEM((2, past_len, C_KV_RANK), F32), pltpu.VMEM((2, C_ROPE_DIM, past_len), F32),
               pltpu.SemaphoreType.DMA((2,)), pltpu.SemaphoreType.DMA((2,))]
    kern = functools.partial(_dec_c_kernel, layer=layer, past_len=past_len)
    return _dec_call(kern, "dec_c", pt, [qc, kvn], [], [cache_lat, cache_krt], 16, C_KV_RANK, scratch)


def _dec_b_kernel(pt_ref, q_ref, kn_ref, vn_ref, slope_ref, ckt_ref, cvt_ref, o_ref,
                  ktbuf, vtbuf, ksem, vsem, m_sc, l_sc, g_sc, o_sc, *, layer, past_len):
    npages = past_len // PAGE
    nblk = past_len // MOBA_BLOCK
    slot = _paged_fetch(pt_ref, layer, [(ckt_ref, ktbuf, ksem, False), (cvt_ref, vtbuf, vsem, False)], npages)
    q = q_ref[...]
    slope = slope_ref[...][:, 0:1]
    kt = ktbuf[slot].astype(BF16)
    vt = vtbuf[slot].astype(BF16)
    s_raw = _dot(q, kt)
    rel = (lax.broadcasted_iota(jnp.int32, (1, past_len), 1) - past_len).astype(F32)
    s = s_raw + slope * rel
    for b in range(nblk):
        lo = b * MOBA_BLOCK
        sb = s[:, lo:lo + MOBA_BLOCK]
        m_b = jnp.max(sb, axis=-1, keepdims=True)
        p = jnp.exp(sb - m_b)
        l_b = jnp.sum(p, axis=-1, keepdims=True)
        gate = jnp.sum(s_raw[:, lo:lo + MOBA_BLOCK], axis=-1, keepdims=True)
        m_sc[b] = jnp.broadcast_to(m_b, (16, 128))
        l_sc[b] = jnp.broadcast_to(l_b, (16, 128))
        g_sc[b] = jnp.broadcast_to(gate, (16, 128))
        o_sc[b] = _dot_nt(p.astype(BF16), vt[:, lo:lo + MOBA_BLOCK])
    gate = g_sc[...]
    blk = lax.broadcasted_iota(jnp.int32, gate.shape, 0)
    sel = jnp.zeros(gate.shape, F32)
    for _ in range(min(MOBA_TOPK, nblk)):
        mx = jnp.max(gate, axis=0, keepdims=True)
        first = jnp.min(jnp.where(gate == mx, blk, nblk), axis=0, keepdims=True)
        pick = blk == first
        sel = jnp.where(pick, 1.0, sel)
        gate = jnp.where(pick, NEG_INF, gate)
    keep = sel > 0.5
    key_t = lax.broadcasted_iota(jnp.int32, (16, 8), 1)
    q_t = lax.broadcasted_iota(jnp.int32, (16, 8), 0) % N_NEW
    s_n = _dot_nt(q, kn_ref[...]) + slope * key_t.astype(F32)
    s_n = jnp.where((key_t <= q_t) & (key_t < N_NEW), s_n, NEG_INF)
    m_n = jnp.max(s_n, axis=-1, keepdims=True)
    m_all = jnp.where(keep, m_sc[...], NEG_INF)
    m_f = jnp.maximum(jnp.max(m_all, axis=0), m_n)
    w = jnp.where(keep, jnp.exp(m_sc[...] - m_f[None]), 0.0)
    p_n = jnp.exp(s_n - m_f[:, 0:1])
    l_f = jnp.sum(w * l_sc[...], axis=0)[:, 0:1] + jnp.sum(p_n, axis=-1, keepdims=True)
    acc = _dot(p_n.astype(BF16), vn_ref[...])
    acc = acc + jnp.sum(jnp.concatenate([w, w], axis=-1) * o_sc[...], axis=0)
    o = acc * (1.0 / l_f)
    rh = lax.broadcasted_iota(jnp.int32, (16, 256), 0) // N_NEW
    ch = lax.broadcasted_iota(jnp.int32, (16, 256), 1) // B_HEAD_DIM
    o_ref[...] = jnp.where(rh == ch, o, 0.0).astype(BF16)


def _dec_b(pt, qbd, slopes, kn, vn, cache_kt, cache_vt, layer, past_len):
    nblk = past_len // MOBA_BLOCK
    scratch = [pltpu.VMEM((2, 256, past_len), F32), pltpu.VMEM((2, 256, past_len), F32),
               pltpu.SemaphoreType.DMA((2,)), pltpu.SemaphoreType.DMA((2,)),
               pltpu.VMEM((nblk, 16, 128), F32), pltpu.VMEM((nblk, 16, 128), F32),
               pltpu.VMEM((nblk, 16, 128), F32), pltpu.VMEM((nblk, 16, 256), F32)]
    kern = functools.partial(_dec_b_kernel, layer=layer, past_len=past_len)
    return _dec_call(kern, "dec_b", pt, [qbd, kn, vn], [slopes], [cache_kt, cache_vt], 16, 256, scratch)


def _decode_attention(page_table, layer, lam_init, lp, subln, aq, akb, avb, bq, bkb, bvb, qc, ckvb,
                      ca_k, ca_v, cb_kt, cb_vt, cc_lat, cc_krt):
    ns = page_table.shape[0]
    n_new = N_NEW
    ts = ns * n_new
    past_len = page_table.shape[1] * PAGE
    sl_a = np.array([SLOPES_A[2 * g + r] for _ in range(2) for g in range(2) for r in range(2)
                     for _ in range(n_new)], dtype=np.float32)
    sl_b = np.array([SLOPES_B[h] for h in range(B_HEADS) for _ in range(n_new)], dtype=np.float32)
    sl_a = jnp.asarray(np.tile(sl_a[:, None], (1, 128)))
    sl_b = jnp.asarray(np.tile(sl_b[:, None], (1, 128)))
    eye2 = jnp.eye(2, dtype=BF16)
    eye4 = jnp.eye(4, dtype=BF16)

    def pad_new(x):
        x = x.reshape(ns, n_new, x.shape[-1])
        return jnp.pad(x, ((0, 0), (0, 8 - n_new), (0, 0)))

    qa = aq.reshape(ns, n_new, 2, 2, 2, A_QK_DIM).transpose(0, 4, 2, 3, 1, 5)
    qa = jnp.einsum('nmgrtd,mM->nmgrtMd', qa, eye2).reshape(ns, 8 * n_new, 128)
    qb = bq.reshape(ns, n_new, B_HEADS, B_HEAD_DIM).transpose(0, 2, 1, 3)
    qb = jnp.einsum('nhtd,hH->nhtHd', qb, eye4).reshape(ns, 4 * n_new, 256)
    qcs = qc.reshape(ns, n_new, C_HEADS, 256).transpose(0, 2, 1, 3).reshape(ns, 4 * n_new, 256)
    oa = _dec_a(page_table, qa, sl_a, akb.reshape(ns, 2 * n_new, 128), avb.reshape(ns, 2 * n_new, 128),
                lp, subln, ca_k, ca_v, layer, lam_init, past_len)
    ob = _dec_b(page_table, qb, sl_b, pad_new(bkb), pad_new(bvb), cb_kt, cb_vt, layer, past_len)
    oc = _dec_c(page_table, qcs, pad_new(ckvb), cc_lat, cc_krt, layer, past_len)
    oa = oa.reshape(ns, A_HEADS, n_new, A_V_DIM).transpose(0, 2, 1, 3).reshape(ts, 512)
    ob = ob.reshape(ns, B_HEADS, n_new, B_HEADS, B_HEAD_DIM)
    ob = jnp.stack([ob[:, hh, :, hh] for hh in range(B_HEADS)], axis=1)
    ob = ob.transpose(0, 2, 1, 3).reshape(ts, 256)
    oc = oc.reshape(ns, C_HEADS, n_new, C_KV_RANK).transpose(0, 2, 1, 3).reshape(ts, 512)
    return oa, ob, oc


def _rope_tables(pos):
    half = C_ROPE_DIM // 2
    inv = ROPE_BASE ** (-jnp.arange(half, dtype=F32) / half)
    ang = pos.astype(F32)[:, None] * inv[None, :]
    cos, sin = jnp.cos(ang), jnp.sin(ang)
    qcs = jnp.tile(cos, (1, 8))
    qsn = jnp.concatenate([jnp.tile(-sin, (1, 4)), jnp.tile(sin, (1, 4))], axis=1)
    pad = jnp.zeros((pos.shape[0], 128 - C_ROPE_DIM), F32)
    kcs = jnp.concatenate([cos, cos, pad], axis=1)
    ksn = jnp.concatenate([-sin, sin, pad], axis=1)
    return qcs, qsn, kcs, ksn


def _uq_perm():
    per = C_NOPE_DIM + C_ROPE_DIM
    half = C_ROPE_DIM // 2
    nope = [h * per + d for h in range(C_HEADS) for d in range(C_NOPE_DIM)]
    r1 = [h * per + C_NOPE_DIM + j for h in range(C_HEADS) for j in range(half)]
    r2 = [h * per + C_NOPE_DIM + half + j for h in range(C_HEADS) for j in range(half)]
    return np.array(nope + r1 + r2, dtype=np.int32)


def _rope_placement():
    half = C_ROPE_DIM // 2
    pm = np.zeros((128, C_HEADS * 256), dtype=np.float32)
    for h in range(C_HEADS):
        for j in range(half):
            pm[h * half + j, h * 256 + C_KV_RANK + j] = 1.0
            pm[64 + h * half + j, h * 256 + C_KV_RANK + half + j] = 1.0
    return pm


def _layer_weights(l, norm_gains, diff_subln, mla_q_norm, mla_w_uq, mla_kv_norm, mla_w_ukv):
    eye = jnp.eye(C_HEADS, dtype=F32)
    w_ukv = mla_w_ukv[l].reshape(C_KV_RANK, C_HEADS, C_NOPE_DIM + C_V_DIM)
    w_uk = w_ukv[..., :C_NOPE_DIM]
    w_uv = w_ukv[..., C_NOPE_DIM:]
    uk_bd = jnp.einsum('chd,hg->hdgc', w_uk, eye)
    uk_bd = jnp.pad(uk_bd, ((0, 0), (0, 0), (0, 0), (0, 256 - C_KV_RANK))).reshape(256, C_HEADS * 256)
    wqc = jnp.concatenate([uk_bd, jnp.asarray(_rope_placement())], axis=0).astype(BF16)
    wuv = jnp.einsum('chv,hg->hcgv', w_uv, eye).reshape(C_HEADS * C_KV_RANK, C_HEADS * C_V_DIM).astype(BF16)
    g = norm_gains[l]
    return dict(
        gains=[g[i][None, :] for i in range(6)],
        subln=diff_subln[l][None, :],
        qn=mla_q_norm[l][None, :], kvn=mla_kv_norm[l][None, :],
        wuq=mla_w_uq[l][:, _uq_perm()].astype(BF16), wqc=wqc, wuv=wuv)


def kernel(x_prompt, x_sample, cache_a_k, cache_a_v, cache_b_k, cache_b_v, cache_c_latent, cache_c_krope,
           page_table, norm_gains, ffn_w_gate_up, ffn_w_down, w_in, w_out, diff_lambda, diff_subln,
           mla_q_norm, mla_w_uq, mla_kv_norm, mla_w_ukv):
    nb, seq, _ = x_prompt.shape
    ns, n_new, _ = x_sample.shape
    depth = w_in.shape[0]
    n_pool, page = cache_a_k.shape[1], cache_a_k.shape[2]
    past_len = page_table.shape[1] * page
    tp, ts = nb * seq, ns * n_new
    assert seq % TQ == 0 and TQ == MOBA_BLOCK and page == PAGE and past_len % MOBA_BLOCK == 0
    assert n_new == N_NEW and depth == 2

    tabs_p = _rope_tables(jnp.arange(seq, dtype=jnp.int32))
    tabs_s = _rope_tables(jnp.tile(past_len + jnp.arange(n_new, dtype=jnp.int32), ns))
    ca_k = cache_a_k.reshape(depth, n_pool, 2 * page, 128)
    ca_v = cache_a_v.reshape(depth, n_pool, 2 * page, 128)
    cb_kt = cache_b_k.transpose(0, 1, 3, 4, 2).reshape(depth, n_pool, 256, page)
    cb_vt = cache_b_v.transpose(0, 1, 3, 4, 2).reshape(depth, n_pool, 256, page)
    cc_krt = cache_c_krope.transpose(0, 1, 3, 2)

    wgu = ffn_w_gate_up.astype(BF16)
    wdn = ffn_w_down.astype(BF16)
    win = jnp.pad(w_in, ((0, 0), (0, 0), (0, IN_PAD - IN_COLS))).astype(BF16)
    wo = w_out.astype(BF16)

    h_p = x_prompt.reshape(tp, D_MODEL)
    h_s = x_sample.reshape(ts, D_MODEL)
    rows_p, rows_s = None, None
    for l in range(depth):
        lam_init = 0.8 - 0.6 * math.exp(-0.3 * l)
        w = _layer_weights(l, norm_gains, diff_subln, mla_q_norm, mla_w_uq, mla_kv_norm, mla_w_ukv)
        gn = w['gains']
        lp = diff_lambda[l]

        h1 = _ffn(h_p, gn[0], gn[1], wgu, wdn, (l, 0), 512)
        (aq, bq, bqf, qc, akb, avb, bkb, bvb, ckvb), rows_p = _proj(
            h1, gn[2], win, l, w['qn'], w['wuq'], w['wqc'], w['kvn'], tabs_p, 512, seq // 512, rows_p)
        oa = _attn_a(aq, akb, avb, lp, w['subln'], lam_init, nb, seq)
        ob = _attn_b(bq, bqf, bkb, rows_p[2], bvb, nb, seq)
        oc = _attn_c(qc, ckvb, nb, seq)
        h2 = _merge(h1, oa, ob, oc, w['wuv'], wo, l, gn[3], 512)
        h_p = _ffn(h2, gn[4], gn[5], wgu, wdn, (l, 1), 512)

        g1 = _ffn(h_s, gn[0], gn[1], wgu, wdn, (l, 0), ts)
        (aq, bq, bqf, qc, akb, avb, bkb, bvb, ckvb), rows_s = _proj(
            g1, gn[2], win, l, w['qn'], w['wuq'], w['wqc'], w['kvn'], tabs_s, ts, 1, rows_s)
        oa, ob, oc = _decode_attention(page_table, l, lam_init, lp, w['subln'], aq, akb, avb, bq, bkb, bvb,
                                       qc, ckvb, ca_k, ca_v, cb_kt, cb_vt, cache_c_latent, cc_krt)
        g2 = _merge(g1, oa, ob, oc, w['wuv'], wo, l, gn[3], ts)
        h_s = _ffn(g2, gn[4], gn[5], wgu, wdn, (l, 1), ts)

    def row_outputs(rows, n, t):
        ak, av, bk, bv, clat, ckr = rows
        return (ak.reshape(depth, n, t, A_KV_HEADS, 2 * A_QK_DIM), av.reshape(depth, n, t, A_KV_HEADS, A_V_DIM),
                bk.reshape(depth, n, t, B_HEADS, B_HEAD_DIM), bv.reshape(depth, n, t, B_HEADS, B_HEAD_DIM),
                clat.reshape(depth, n, t, C_KV_RANK), ckr.reshape(depth, n, t, C_ROPE_DIM))

    return (h_p.reshape(nb, seq, D_MODEL), h_s.reshape(ns, n_new, D_MODEL),
            *row_outputs(rows_p, nb, seq), *row_outputs(rows_s, ns, n_new))
```

```python
import functools
import math

import numpy as np
import jax
import jax.numpy as jnp
from jax import lax
from jax.experimental import pallas as pl
from jax.experimental.pallas import tpu as pltpu

F32 = jnp.float32
BF16 = jnp.bfloat16

D_MODEL = 1024
D_FF = 2816
A_HEADS = 4
A_KV_HEADS = 2
A_QK_DIM = 64
A_V_DIM = 128
B_HEADS = 4
B_HEAD_DIM = 64
MOBA_BLOCK = 256
MOBA_TOPK = 3
C_HEADS = 4
C_Q_RANK = 256
C_KV_RANK = 128
C_NOPE_DIM = 64
C_ROPE_DIM = 32
C_V_DIM = 64
C_SCALE = (C_NOPE_DIM + C_ROPE_DIM) ** -0.5
ROPE_BASE = 10000.0
NORM_EPS = 1e-6
QK_SCALE = 0.125

IN_COLS = 2208
IN_PAD = 2304
TQ = 256
PAGE = 128
NEG_INF = float("-inf")

VMEM_LIMIT = 56 * 1024 * 1024

SLOPES_A = tuple(2.0 ** -(2 * h + 1) for h in range(A_HEADS))
SLOPES_B = tuple(2.0 ** -(2 * h + 2) for h in range(B_HEADS))


def _rms(x, g):
    return x * lax.rsqrt(jnp.mean(x * x, axis=-1, keepdims=True) + NORM_EPS) * g


def _dot(a, b):
    return jnp.dot(a, b, preferred_element_type=F32)


def _dot_nt(a, b, precision=None):
    return lax.dot_general(a, b, (((1,), (1,)), ((), ())), preferred_element_type=F32,
                           precision=precision)


def _const_spec(shape):
    nd = len(shape)
    return pl.BlockSpec(shape, lambda *_: (0,) * nd, pipeline_mode=pl.Buffered(1))


def _stacked_spec(shape, lead):
    nd = len(shape)
    return pl.BlockSpec((None,) * len(lead) + tuple(shape), lambda *_: tuple(lead) + (0,) * nd,
                        pipeline_mode=pl.Buffered(1))


def _params(sem):
    return pltpu.CompilerParams(dimension_semantics=sem, vmem_limit_bytes=VMEM_LIMIT)


FFN_CHUNKS = 2
FFN_CHUNK = D_FF // FFN_CHUNKS


def _ffn_body(x, gpre, gpost, wgu_ref, wdn_ref):
    xn = _rms(x, gpre).astype(BF16)
    acc = None
    for c in range(FFN_CHUNKS):
        lo = c * FFN_CHUNK
        g = _dot(xn, wgu_ref[:, lo:lo + FFN_CHUNK])
        u = _dot(xn, wgu_ref[:, D_FF + lo:D_FF + lo + FFN_CHUNK])
        a = (g * (1.0 / (1.0 + jnp.exp(-g))) * u).astype(BF16)
        part = _dot(a, wdn_ref[lo:lo + FFN_CHUNK, :])
        acc = part if acc is None else acc + part
    return x + 0.5 * _rms(acc, gpost)


def _ffn_kernel(x_ref, gpre_ref, gpost_ref, wgu_ref, wdn_ref, o_ref):
    o_ref[...] = _ffn_body(x_ref[...], gpre_ref[...], gpost_ref[...], wgu_ref, wdn_ref)


def _ffn(x, gpre, gpost, wgu, wdn, lead, tm):
    t = x.shape[0]
    return pl.pallas_call(
        _ffn_kernel,
        out_shape=jax.ShapeDtypeStruct((t, D_MODEL), F32),
        grid=(t // tm,),
        in_specs=[pl.BlockSpec((tm, D_MODEL), lambda i: (i, 0)),
                  _const_spec((1, D_MODEL)), _const_spec((1, D_MODEL)),
                  _stacked_spec((D_MODEL, 2 * D_FF), lead), _stacked_spec((D_FF, D_MODEL), lead)],
        out_specs=pl.BlockSpec((tm, D_MODEL), lambda i: (i, 0)),
        compiler_params=_params(("parallel",)),
        name="ffn",
    )(x, gpre, gpost, wgu, wdn)


N_PROJ_IN = 11
N_WORK_OUT = 9
N_ROW_OUT = 6


def _proj_kernel(*refs, n_prev):
    (h_ref, g_ref, win_ref, qn_ref, wuq_ref, wqc_ref, kvn_ref,
     qcs_ref, qsn_ref, kcs_ref, ksn_ref) = refs[:N_PROJ_IN]
    prev_refs = refs[N_PROJ_IN:N_PROJ_IN + n_prev]
    (aq_ref, bq_ref, bqf_ref, qc_ref, akb_ref, avb_ref, bkb_ref, bvb_ref, ckvb_ref) = \
        refs[N_PROJ_IN + n_prev:N_PROJ_IN + n_prev + N_WORK_OUT]
    row_refs = refs[N_PROJ_IN + n_prev + N_WORK_OUT:]
    if n_prev:
        for prev_ref, row_ref in zip(prev_refs, row_refs):
            row_ref[0] = prev_ref[...]
        row_refs = [r.at[1] for r in row_refs]
    ak_ref, av_ref, bk_ref, bv_ref, clat_ref, ckr_ref = row_refs
    tm = h_ref.shape[0]
    u = _rms(h_ref[...], g_ref[...]).astype(BF16)
    z = _dot(u, win_ref[...])
    aq_ref[...] = (z[:, 0:512] * QK_SCALE).astype(BF16)
    ak = z[:, 512:768]
    akb_ref[...] = ak.astype(BF16)
    av = z[:, 768:1024]
    avb_ref[...] = av.astype(BF16)
    for g in range(A_KV_HEADS):
        ak_ref[pl.ds(g, tm, stride=A_KV_HEADS), :] = ak[:, g * 128:(g + 1) * 128]
        av_ref[pl.ds(g, tm, stride=A_KV_HEADS), :] = av[:, g * 128:(g + 1) * 128]
    bq = z[:, 1024:1280]
    bqf_ref[...] = bq
    bq_ref[...] = (bq * QK_SCALE).astype(BF16)
    bk = z[:, 1280:1536]
    bk_ref[...] = bk
    bkb_ref[...] = bk.astype(BF16)
    bv = z[:, 1536:1792]
    bv_ref[...] = bv
    bvb_ref[...] = bv.astype(BF16)
    cqn = _rms(z[:, 1792:2048], qn_ref[...]).astype(BF16)
    q = _dot(cqn, wuq_ref[...])
    qr = q[:, 256:384]
    qrot = qr * qcs_ref[...] + pltpu.roll(qr, 64, 1) * qsn_ref[...]
    qcat = jnp.concatenate([q[:, 0:256], qrot], axis=1).astype(BF16)
    qc_ref[...] = _dot(qcat, wqc_ref[...]).astype(BF16)
    clat = _rms(z[:, 2048:2176], kvn_ref[...])
    clat_ref[...] = clat
    kr = z[:, 2176:2304]
    lane = lax.broadcasted_iota(jnp.int32, kr.shape, 1)
    swapped = jnp.where(lane < 16, pltpu.roll(kr, 112, 1), pltpu.roll(kr, 16, 1))
    krot = kr * kcs_ref[...] + swapped * ksn_ref[...]
    ckr_ref[...] = krot[:, 0:C_ROPE_DIM]
    ckvb_ref[...] = jnp.concatenate([clat, krot], axis=1).astype(BF16)


def _proj(h, g, win, layer, qn, wuq, wqc, kvn, tabs, tm, tab_blocks, prev_rows):
    t = h.shape[0]
    row = lambda w: pl.BlockSpec((tm, w), lambda i: (i, 0))
    tab = pl.BlockSpec((tm, 128), lambda i: (i % tab_blocks, 0))
    work = [(512, BF16), (256, BF16), (256, F32), (1024, BF16),
            (256, BF16), (256, BF16), (256, BF16), (256, BF16), (256, BF16)]
    assert len(work) == N_WORK_OUT
    rows = [(A_KV_HEADS, 128), (A_KV_HEADS, 128), (1, 256), (1, 256), (1, C_KV_RANK), (1, C_ROPE_DIM)]
    if prev_rows is None:
        n_prev = 0
        row_shapes = [jax.ShapeDtypeStruct((r * t, w), F32) for r, w in rows]
        row_specs = [pl.BlockSpec((r * tm, w), lambda i: (i, 0)) for r, w in rows]
        prev_specs = []
    else:
        n_prev = N_ROW_OUT
        row_shapes = [jax.ShapeDtypeStruct((2, r * t, w), F32) for r, w in rows]
        row_specs = [pl.BlockSpec((2, r * tm, w), lambda i: (0, i, 0)) for r, w in rows]
        prev_specs = [pl.BlockSpec((r * tm, w), lambda i: (i, 0)) for r, w in rows]
    out = pl.pallas_call(
        functools.partial(_proj_kernel, n_prev=n_prev),
        out_shape=[jax.ShapeDtypeStruct((t, w), d) for w, d in work] + row_shapes,
        grid=(t // tm,),
        in_specs=[row(D_MODEL), _const_spec((1, D_MODEL)), _stacked_spec((D_MODEL, IN_PAD), (layer,)),
                  _const_spec((1, C_Q_RANK)), _const_spec((C_Q_RANK, 384)), _const_spec((384, 1024)),
                  _const_spec((1, C_KV_RANK)), tab, tab, tab, tab] + prev_specs,
        out_specs=[row(w) for w, _ in work] + row_specs,
        compiler_params=_params(("parallel",)),
        name="proj",
    )(h, g, win, qn, wuq, wqc, kvn, *tabs, *([] if prev_rows is None else prev_rows))
    return out[:N_WORK_OUT], out[N_WORK_OUT:]


def _lambda_value(lp, lam_init):
    return (jnp.exp(jnp.sum(lp[0:1, :] * lp[1:2, :], axis=-1, keepdims=True))
            - jnp.exp(jnp.sum(lp[2:3, :] * lp[3:4, :], axis=-1, keepdims=True)) + lam_init)


def _causal_tile(reps):
    row = lax.broadcasted_iota(jnp.int32, (TQ, TQ), 0)
    col = lax.broadcasted_iota(jnp.int32, (TQ, TQ), 1)
    m = col <= row
    return jnp.concatenate([m] * reps, axis=0)


def _softmax_parts(s_p, s_d):
    m = jnp.max(s_d, axis=-1, keepdims=True)
    if s_p is not None:
        m = jnp.maximum(m, jnp.max(s_p, axis=-1, keepdims=True))
    p_d = jnp.exp(s_d - m)
    l = jnp.sum(p_d, axis=-1, keepdims=True)
    p_p = None
    if s_p is not None:
        p_p = jnp.exp(s_p - m)
        l = l + jnp.sum(p_p, axis=-1, keepdims=True)
    return p_p, p_d, l


def _attn_a_kernel(q_ref, k_ref, kpos_ref, v_ref, lp_ref, subln_ref, o_ref, *, lam_init, seq):
    g = pl.program_id(1)
    lam = _lambda_value(lp_ref[...], lam_init)
    lane = lax.broadcasted_iota(jnp.int32, (TQ, 128), 1)
    is_m0 = lane < A_QK_DIM
    causal = _causal_tile(4)
    rowblk = lax.broadcasted_iota(jnp.int32, (4 * TQ, 1), 0) // TQ
    s_r0 = jnp.where(g == 0, SLOPES_A[0], SLOPES_A[2])
    s_r1 = jnp.where(g == 0, SLOPES_A[1], SLOPES_A[3])
    slope = jnp.where(rowblk % 2 == 0, s_r0, s_r1).astype(F32)
    coef_sign = jnp.where(rowblk < 2, 1.0, -lam)
    q_pos = jnp.where(lax.broadcasted_iota(jnp.int32, (4 * TQ, 128), 1) < 2, slope, 0.0).astype(BF16)
    kk = jnp.concatenate([k_ref[...], kpos_ref[...]], axis=1)
    for qi in range(seq // TQ):
        r0 = qi * TQ
        q0 = q_ref[r0:r0 + TQ, 0:128]
        q1 = q_ref[r0:r0 + TQ, 128:256]
        zero = jnp.zeros_like(q0)
        qz = jnp.concatenate([jnp.where(is_m0, q0, zero), jnp.where(is_m0, q1, zero),
                              jnp.where(is_m0, zero, q0), jnp.where(is_m0, zero, q1)], axis=0)
        qz = jnp.concatenate([qz, q_pos], axis=1)
        s_d = jnp.where(causal, _dot_nt(qz, kk[r0:r0 + TQ, :]), NEG_INF)
        s_p = _dot_nt(qz, kk[0:r0, :]) if qi > 0 else None
        p_p, p_d, l = _softmax_parts(s_p, s_d)
        coef = coef_sign / l
        w_d = p_d * coef
        pd_d = (w_d[0:2 * TQ] + w_d[2 * TQ:4 * TQ]).astype(BF16)
        o = _dot(pd_d, v_ref[r0:r0 + TQ, :])
        if qi > 0:
            w_p = p_p * coef
            pd_p = (w_p[0:2 * TQ] + w_p[2 * TQ:4 * TQ]).astype(BF16)
            o = o + _dot(pd_p, v_ref[0:r0, :])
        y = _rms(o, subln_ref[...]) * (1.0 - lam_init)
        o_ref[r0:r0 + TQ, 0:128] = y[0:TQ].astype(BF16)
        o_ref[r0:r0 + TQ, 128:256] = y[TQ:2 * TQ].astype(BF16)


def _key_position_columns(seq):
    pos = np.arange(seq)
    cols = np.zeros((seq, 128), dtype=np.float32)
    cols[:, 0] = (pos // 256) * 256
    cols[:, 1] = pos % 256
    return jnp.asarray(cols).astype(BF16)


def _attn_a(aq, akb, avb, lp, subln, lam_init, nb, seq):
    return pl.pallas_call(
        functools.partial(_attn_a_kernel, lam_init=lam_init, seq=seq),
        out_shape=jax.ShapeDtypeStruct((nb * seq, 512), BF16),
        grid=(nb, A_KV_HEADS),
        in_specs=[pl.BlockSpec((seq, 256), lambda n, g: (n, g)),
                  pl.BlockSpec((seq, 128), lambda n, g: (n, g)),
                  _const_spec((seq, 128)),
                  pl.BlockSpec((seq, 128), lambda n, g: (n, g)),
                  _const_spec((4, A_QK_DIM)), _const_spec((1, A_V_DIM))],
        out_specs=pl.BlockSpec((seq, 256), lambda n, g: (n, g)),
        compiler_params=_params(("parallel", "parallel")),
        name="attn_a",
    )(aq, akb, _key_position_columns(seq), avb, lp, subln)


def _attn_c_kernel(q_ref, kv_ref, o_ref, *, seq):
    causal = _causal_tile(C_HEADS)
    for qi in range(seq // TQ):
        r0 = qi * TQ
        q4 = jnp.concatenate([q_ref[r0:r0 + TQ, h * 256:(h + 1) * 256] for h in range(C_HEADS)], axis=0)
        s_d = jnp.where(causal, _dot_nt(q4, kv_ref[r0:r0 + TQ, :]) * C_SCALE, NEG_INF)
        s_p = _dot_nt(q4, kv_ref[0:r0, :]) * C_SCALE if qi > 0 else None
        p_p, p_d, l = _softmax_parts(s_p, s_d)
        o = _dot(p_d.astype(BF16), kv_ref[r0:r0 + TQ, 0:C_KV_RANK])
        if qi > 0:
            o = o + _dot(p_p.astype(BF16), kv_ref[0:r0, 0:C_KV_RANK])
        o = o * (1.0 / l)
        for h in range(C_HEADS):
            o_ref[r0:r0 + TQ, h * 128:(h + 1) * 128] = o[h * TQ:(h + 1) * TQ].astype(BF16)


def _attn_c(qc, ckvb, nb, seq):
    return pl.pallas_call(
        functools.partial(_attn_c_kernel, seq=seq),
        out_shape=jax.ShapeDtypeStruct((nb * seq, C_HEADS * C_KV_RANK), BF16),
        grid=(nb,),
        in_specs=[pl.BlockSpec((seq, 1024), lambda n: (n, 0)),
                  pl.BlockSpec((seq, 256), lambda n: (n, 0))],
        out_specs=pl.BlockSpec((seq, C_HEADS * C_KV_RANK), lambda n: (n, 0)),
        compiler_params=_params(("parallel",)),
        name="attn_c",
    )(qc, ckvb)


def _head_masks(rows):
    lane = lax.broadcasted_iota(jnp.int32, (rows, B_HEADS * B_HEAD_DIM), 1)
    return [(lane >= h * B_HEAD_DIM) & (lane < (h + 1) * B_HEAD_DIM) for h in range(B_HEADS)]


def _attn_b_kernel(q_ref, qf_ref, k_ref, kf_ref, v_ref, o_ref, *, seq):
    nblk = seq // MOBA_BLOCK
    hm = _head_masks(TQ)
    causal = _causal_tile(B_HEADS)
    rowblk = lax.broadcasted_iota(jnp.int32, (B_HEADS * TQ, 1), 0) // TQ
    slope = jnp.where(rowblk == 0, SLOPES_B[0],
                      jnp.where(rowblk == 1, SLOPES_B[1],
                                jnp.where(rowblk == 2, SLOPES_B[2], SLOPES_B[3]))).astype(F32)
    rel_d = lax.broadcasted_iota(jnp.int32, (1, TQ), 1).astype(F32)
    kmean = jnp.concatenate(
        [jnp.sum(kf_ref[b * MOBA_BLOCK:(b + 1) * MOBA_BLOCK, :], axis=0, keepdims=True) for b in range(nblk)],
        axis=0) * (1.0 / MOBA_BLOCK)
    blk = lax.broadcasted_iota(jnp.int32, (B_HEADS * TQ, nblk), 1)
    for qi in range(nblk):
        r0 = qi * TQ
        q = q_ref[r0:r0 + TQ, :]
        qz = jnp.concatenate([jnp.where(hm[h], q, jnp.zeros_like(q)) for h in range(B_HEADS)], axis=0)
        s_d = _dot_nt(qz, k_ref[r0:r0 + TQ, :]) + slope * rel_d
        s_d = jnp.where(causal, s_d, NEG_INF)
        s_p = None
        if qi > 0:
            rel_p = (lax.broadcasted_iota(jnp.int32, (1, r0), 1) - r0).astype(F32)
            s_p = _dot_nt(qz, k_ref[0:r0, :]) + slope * rel_p
            if qi > MOBA_TOPK:
                qf = qf_ref[r0:r0 + TQ, :]
                qfz = jnp.concatenate([jnp.where(hm[h], qf, jnp.zeros_like(qf)) for h in range(B_HEADS)],
                                      axis=0)
                gate = _dot_nt(qfz, kmean, precision=lax.Precision.HIGHEST)
                gate = jnp.where(blk < qi, gate, NEG_INF)
                picks = []
                for _ in range(MOBA_TOPK):
                    mx = jnp.max(gate, axis=-1, keepdims=True)
                    first = jnp.min(jnp.where(gate == mx, blk, nblk), axis=-1, keepdims=True)
                    picks.append(first)
                    gate = jnp.where(blk == first, NEG_INF, gate)
                pieces = []
                for b in range(qi):
                    keep = (picks[0] == b) | (picks[1] == b) | (picks[2] == b)
                    pieces.append(jnp.where(keep, s_p[:, b * MOBA_BLOCK:(b + 1) * MOBA_BLOCK], NEG_INF))
                s_p = jnp.concatenate(pieces, axis=1)
        p_p, p_d, l = _softmax_parts(s_p, s_d)
        o = _dot(p_d.astype(BF16), v_ref[r0:r0 + TQ, :])
        if qi > 0:
            o = o + _dot(p_p.astype(BF16), v_ref[0:r0, :])
        o = o * (1.0 / l)
        out = jnp.where(hm[0], o[0:TQ], 0.0)
        for h in range(1, B_HEADS):
            out = out + jnp.where(hm[h], o[h * TQ:(h + 1) * TQ], 0.0)
        o_ref[r0:r0 + TQ, :] = out.astype(BF16)


def _attn_b(bq, bqf, bkb, bk, bvb, nb, seq):
    spec = pl.BlockSpec((seq, 256), lambda n: (n, 0))
    bk_spec = spec if bk.ndim == 2 else pl.BlockSpec((None, seq, 256), lambda n: (bk.shape[0] - 1, n, 0))
    return pl.pallas_call(
        functools.partial(_attn_b_kernel, seq=seq),
        out_shape=jax.ShapeDtypeStruct((nb * seq, 256), BF16),
        grid=(nb,),
        in_specs=[spec, spec, spec, bk_spec, spec],
        out_specs=spec,
        compiler_params=_params(("parallel",)),
        name="attn_b",
    )(bq, bqf, bkb, bk, bvb)


def _merge_kernel(h_ref, oa_ref, ob_ref, oc_ref, wuv_ref, wo_ref, g_ref, o_ref):
    oc = _dot(oc_ref[...], wuv_ref[...]).astype(BF16)
    mix = (_dot(oa_ref[...], wo_ref[0:512, :]) + _dot(ob_ref[...], wo_ref[512:768, :])
           + _dot(oc, wo_ref[768:1024, :]))
    o_ref[...] = h_ref[...] + _rms(mix, g_ref[...])


def _merge(h, oa, ob, oc, wuv, wo, layer, g, tm):
    t = h.shape[0]
    row = lambda w: pl.BlockSpec((tm, w), lambda i: (i, 0))
    return pl.pallas_call(
        _merge_kernel,
        out_shape=jax.ShapeDtypeStruct((t, D_MODEL), F32),
        grid=(t // tm,),
        in_specs=[row(D_MODEL), row(512), row(256), row(512),
                  _const_spec((512, 256)), _stacked_spec((D_MODEL, D_MODEL), (layer,)),
                  _const_spec((1, D_MODEL))],
        out_specs=row(D_MODEL),
        compiler_params=_params(("parallel",)),
        name="merge",
    )(h, oa, ob, oc, wuv, wo, g)


N_NEW = 4
DEC_VMEM_LIMIT = 58 * 1024 * 1024
ISSUE_UNROLL = 8


def _page_copy(pt_ref, seq, j, layer, cache_ref, buf_ref, slot, sem, stack_rows):
    src = cache_ref.at[layer, pt_ref[seq, j]]
    if stack_rows:
        r = cache_ref.shape[2]
        dst = buf_ref.at[slot, pl.ds(pl.multiple_of(j * r, r), r), :]
    else:
        dst = buf_ref.at[slot, :, pl.ds(pl.multiple_of(j * PAGE, PAGE), PAGE)]
    return pltpu.make_async_copy(src, dst, sem)


def _paged_fetch(pt_ref, layer, streams, npages):
    n = pl.program_id(0)
    slot = n % 2

    def for_all_pages(seq, sl, act):
        def body(j, carry):
            for k, (cache_ref, buf_ref, sem_ref, stack_rows) in enumerate(streams):
                act(_page_copy(pt_ref, seq, j, layer, cache_ref, buf_ref, sl, sem_ref.at[sl], stack_rows), k)
            return carry
        lax.fori_loop(0, npages, body, 0, unroll=ISSUE_UNROLL)

    start = lambda cp, k: cp.start(priority=k % 2)

    @pl.when(n == 0)
    def _():
        for_all_pages(0, 0, start)

    @pl.when(n + 1 < pl.num_programs(0))
    def _():
        for_all_pages(n + 1, 1 - slot, start)

    for_all_pages(n, slot, lambda cp, k: cp.wait())
    return slot


def _dec_a_kernel(pt_ref, q_ref, kn_ref, vn_ref, slope_ref, lp_ref, subln_ref, ck_ref, cv_ref,
                  o_ref, kbuf, vbuf, ksem, vsem, *, layer, lam_init, past_len):
    npages = past_len // PAGE
    slot = _paged_fetch(pt_ref, layer, [(ck_ref, kbuf, ksem, True), (cv_ref, vbuf, vsem, True)], npages)
    rows = 2 * past_len
    q = q_ref[...]
    slope = slope_ref[...][:, 0:1]
    row_g = (lax.broadcasted_iota(jnp.int32, (32, 1), 0) // (2 * N_NEW)) % 2
    kc = kbuf[slot].astype(BF16)
    vc = vbuf[slot].astype(BF16)
    col = lax.broadcasted_iota(jnp.int32, (1, rows), 1)
    rel = ((col >> 1) - past_len).astype(F32)
    s = _dot_nt(q, kc) + slope * rel
    s = jnp.where((col & 1) == row_g, s, NEG_INF)
    ncol = lax.broadcasted_iota(jnp.int32, (32, 2 * N_NEW), 1)
    q_t = lax.broadcasted_iota(jnp.int32, (32, 2 * N_NEW), 0) % N_NEW
    s_n = _dot_nt(q, kn_ref[...]) + slope * (ncol >> 1).astype(F32)
    s_n = jnp.where(((ncol & 1) == row_g) & ((ncol >> 1) <= q_t), s_n, NEG_INF)
    m = jnp.maximum(jnp.max(s, axis=-1, keepdims=True), jnp.max(s_n, axis=-1, keepdims=True))
    p = jnp.exp(s - m)
    p_n = jnp.exp(s_n - m)
    l = jnp.sum(p, axis=-1, keepdims=True) + jnp.sum(p_n, axis=-1, keepdims=True)
    o = (_dot(p.astype(BF16), vc) + _dot(p_n.astype(BF16), vn_ref[...])) * (1.0 / l)
    lam = _lambda_value(lp_ref[...], lam_init)
    pd = o[0:16] - lam * o[16:32]
    o_ref[...] = (_rms(pd, subln_ref[...]) * (1.0 - lam_init)).astype(BF16)


def _dec_call(kernel_fn, name, pt, per_seq_inputs, const_inputs, caches, out_rows, out_cols, scratch):
    nseq = pt.shape[0]
    per_seq = lambda a: pl.BlockSpec((None,) + a.shape[1:], lambda n, pt: (n, 0, 0))
    const = lambda a: pl.BlockSpec(a.shape, lambda n, pt: (0, 0))
    grid_spec = pltpu.PrefetchScalarGridSpec(
        num_scalar_prefetch=1,
        grid=(nseq,),
        in_specs=[per_seq(a) if a.ndim == 3 else const(a) for a in per_seq_inputs + const_inputs]
                 + [pl.BlockSpec(memory_space=pl.ANY)] * len(caches),
        out_specs=pl.BlockSpec((None, out_rows, out_cols), lambda n, pt: (n, 0, 0)),
        scratch_shapes=scratch,
    )
    return pl.pallas_call(
        kernel_fn,
        out_shape=jax.ShapeDtypeStruct((nseq, out_rows, out_cols), BF16),
        grid_spec=grid_spec,
        compiler_params=pltpu.CompilerParams(dimension_semantics=("arbitrary",),
                                             vmem_limit_bytes=DEC_VMEM_LIMIT),
        name=name,
    )(pt, *per_seq_inputs, *const_inputs, *caches)


def _dec_a(pt, qa, slopes, kn, vn, lp, subln, cache_k, cache_v, layer, lam_init, past_len):
    rows = 2 * past_len
    scratch = [pltpu.VMEM((2, rows, 128), F32), pltpu.VMEM((2, rows, 128), F32),
               pltpu.SemaphoreType.DMA((2,)), pltpu.SemaphoreType.DMA((2,))]
    kern = functools.partial(_dec_a_kernel, layer=layer, lam_init=lam_init, past_len=past_len)
    return _dec_call(kern, "dec_a", pt, [qa, kn, vn], [slopes, lp, subln], [cache_k, cache_v],
                     16, A_V_DIM, scratch)


def _dec_c_kernel(pt_ref, q_ref, kvn_ref, clat_ref, ckrt_ref, o_ref, latbuf, krbuf, lsem, rsem,
                  *, layer, past_len):
    npages = past_len // PAGE
    slot = _paged_fetch(pt_ref, layer, [(clat_ref, latbuf, lsem, True), (ckrt_ref, krbuf, rsem, False)], npages)
    q = q_ref[...]
    q_lat = q[:, 0:C_KV_RANK]
    q_rope = q[:, C_KV_RANK:C_KV_RANK + C_ROPE_DIM]
    lat = latbuf[slot].astype(BF16)
    krt = krbuf[slot].astype(BF16)
    s = (_dot_nt(q_lat, lat) + _dot(q_rope, krt)) * C_SCALE
    kvn = kvn_ref[...]
    key_t = lax.broadcasted_iota(jnp.int32, (16, 8), 1)
    q_t = lax.broadcasted_iota(jnp.int32, (16, 8), 0) % N_NEW
    s_n = _dot_nt(q, kvn) * C_SCALE
    s_n = jnp.where((key_t <= q_t) & (key_t < N_NEW), s_n, NEG_INF)
    m = jnp.maximum(jnp.max(s, axis=-1, keepdims=True), jnp.max(s_n, axis=-1, keepdims=True))
    p = jnp.exp(s - m)
    p_n = jnp.exp(s_n - m)
    l = jnp.sum(p, axis=-1, keepdims=True) + jnp.sum(p_n, axis=-1, keepdims=True)
    o = _dot(p.astype(BF16), lat) + _dot(p_n.astype(BF16), kvn[:, 0:C_KV_RANK])
    o_ref[...] = (o * (1.0 / l)).astype(BF16)


def _dec_c(pt, qc, kvn, cache_lat, cache_krt, layer, past_len):
    scratch = [pltpu.VMEM((2, past_len, C_KV_RANK), F32), pltpu.VMEM((2, C_ROPE_DIM, past_len), F32),
               pltpu.SemaphoreType.DMA((2,)), pltpu.SemaphoreType.DMA((2,))]
    kern = functools.partial(_dec_c_kernel, layer=layer, past_len=past_len)
    return _dec_call(kern, "dec_c", pt, [qc, kvn], [], [cache_lat, cache_krt], 16, C_KV_RANK, scratch)


def _dec_b_kernel(pt_ref, q_ref, kn_ref, vn_ref, slope_ref, ckt_ref, cvt_ref, o_ref,
                  ktbuf, vtbuf, ksem, vsem, m_sc, l_sc, g_sc, o_sc, *, layer, past_len):
    npages = past_len // PAGE
    nblk = past_len // MOBA_BLOCK
    slot = _paged_fetch(pt_ref, layer, [(ckt_ref, ktbuf, ksem, False), (cvt_ref, vtbuf, vsem, False)], npages)
    q = q_ref[...]
    slope = slope_ref[...][:, 0:1]
    kt = ktbuf[slot].astype(BF16)
    vt = vtbuf[slot].astype(BF16)
    s_raw = _dot(q, kt)
    rel = (lax.broadcasted_iota(jnp.int32, (1, past_len), 1) - past_len).astype(F32)
    s = s_raw + slope * rel
    for b in range(nblk):
        lo = b * MOBA_BLOCK
        sb = s[:, lo:lo + MOBA_BLOCK]
        m_b = jnp.max(sb, axis=-1, keepdims=True)
        p = jnp.exp(sb - m_b)
        l_b = jnp.sum(p, axis=-1, keepdims=True)
        gate = jnp.sum(s_raw[:, lo:lo + MOBA_BLOCK], axis=-1, keepdims=True)
        m_sc[b] = jnp.broadcast_to(m_b, (16, 128))
        l_sc[b] = jnp.broadcast_to(l_b, (16, 128))
        g_sc[b] = jnp.broadcast_to(gate, (16, 128))
        o_sc[b] = _dot_nt(p.astype(BF16), vt[:, lo:lo + MOBA_BLOCK])
    gate = g_sc[...]
    blk = lax.broadcasted_iota(jnp.int32, gate.shape, 0)
    sel = jnp.zeros(gate.shape, F32)
    for _ in range(min(MOBA_TOPK, nblk)):
        mx = jnp.max(gate, axis=0, keepdims=True)
        first = jnp.min(jnp.where(gate == mx, blk, nblk), axis=0, keepdims=True)
        pick = blk == first
        sel = jnp.where(pick, 1.0, sel)
        gate = jnp.where(pick, NEG_INF, gate)
    keep = sel > 0.5
    key_t = lax.broadcasted_iota(jnp.int32, (16, 8), 1)
    q_t = lax.broadcasted_iota(jnp.int32, (16, 8), 0) % N_NEW
    s_n = _dot_nt(q, kn_ref[...]) + slope * key_t.astype(F32)
    s_n = jnp.where((key_t <= q_t) & (key_t < N_NEW), s_n, NEG_INF)
    m_n = jnp.max(s_n, axis=-1, keepdims=True)
    m_all = jnp.where(keep, m_sc[...], NEG_INF)
    m_f = jnp.maximum(jnp.max(m_all, axis=0), m_n)
    w = jnp.where(keep, jnp.exp(m_sc[...] - m_f[None]), 0.0)
    p_n = jnp.exp(s_n - m_f[:, 0:1])
    l_f = jnp.sum(w * l_sc[...], axis=0)[:, 0:1] + jnp.sum(p_n, axis=-1, keepdims=True)
    acc = _dot(p_n.astype(BF16), vn_ref[...])
    acc = acc + jnp.sum(jnp.concatenate([w, w], axis=-1) * o_sc[...], axis=0)
    o = acc * (1.0 / l_f)
    rh = lax.broadcasted_iota(jnp.int32, (16, 256), 0) // N_NEW
    ch = lax.broadcasted_iota(jnp.int32, (16, 256), 1) // B_HEAD_DIM
    o_ref[...] = jnp.where(rh == ch, o, 0.0).astype(BF16)


def _dec_b(pt, qbd, slopes, kn, vn, cache_kt, cache_vt, layer, past_len):
    nblk = past_len // MOBA_BLOCK
    scratch = [pltpu.VMEM((2, 256, past_len), F32), pltpu.VMEM((2, 256, past_len), F32),
               pltpu.SemaphoreType.DMA((2,)), pltpu.SemaphoreType.DMA((2,)),
               pltpu.VMEM((nblk, 16, 128), F32), pltpu.VMEM((nblk, 16, 128), F32),
               pltpu.VMEM((nblk, 16, 128), F32), pltpu.VMEM((nblk, 16, 256), F32)]
    kern = functools.partial(_dec_b_kernel, layer=layer, past_len=past_len)
    return _dec_call(kern, "dec_b", pt, [qbd, kn, vn], [slopes], [cache_kt, cache_vt], 16, 256, scratch)


def _decode_attention(page_table, layer, lam_init, lp, subln, aq, akb, avb, bq, bkb, bvb, qc, ckvb,
                      ca_k, ca_v, cb_kt, cb_vt, cc_lat, cc_krt):
    ns = page_table.shape[0]
    n_new = N_NEW
    ts = ns * n_new
    past_len = page_table.shape[1] * PAGE
    sl_a = np.array([SLOPES_A[2 * g + r] for _ in range(2) for g in range(2) for r in range(2)
                     for _ in range(n_new)], dtype=np.float32)
    sl_b = np.array([SLOPES_B[h] for h in range(B_HEADS) for _ in range(n_new)], dtype=np.float32)
    sl_a = jnp.asarray(np.tile(sl_a[:, None], (1, 128)))
    sl_b = jnp.asarray(np.tile(sl_b[:, None], (1, 128)))
    eye2 = jnp.eye(2, dtype=BF16)
    eye4 = jnp.eye(4, dtype=BF16)

    def pad_new(x):
        x = x.reshape(ns, n_new, x.shape[-1])
        return jnp.pad(x, ((0, 0), (0, 8 - n_new), (0, 0)))

    qa = aq.reshape(ns, n_new, 2, 2, 2, A_QK_DIM).transpose(0, 4, 2, 3, 1, 5)
    qa = jnp.einsum('nmgrtd,mM->nmgrtMd', qa, eye2).reshape(ns, 8 * n_new, 128)
    qb = bq.reshape(ns, n_new, B_HEADS, B_HEAD_DIM).transpose(0, 2, 1, 3)
    qb = jnp.einsum('nhtd,hH->nhtHd', qb, eye4).reshape(ns, 4 * n_new, 256)
    qcs = qc.reshape(ns, n_new, C_HEADS, 256).transpose(0, 2, 1, 3).reshape(ns, 4 * n_new, 256)
    oa = _dec_a(page_table, qa, sl_a, akb.reshape(ns, 2 * n_new, 128), avb.reshape(ns, 2 * n_new, 128),
                lp, subln, ca_k, ca_v, layer, lam_init, past_len)
    ob = _dec_b(page_table, qb, sl_b, pad_new(bkb), pad_new(bvb), cb_kt, cb_vt, layer, past_len)
    oc = _dec_c(page_table, qcs, pad_new(ckvb), cc_lat, cc_krt, layer, past_len)
    oa = oa.reshape(ns, A_HEADS, n_new, A_V_DIM).transpose(0, 2, 1, 3).reshape(ts, 512)
    ob = ob.reshape(ns, B_HEADS, n_new, B_HEADS, B_HEAD_DIM)
    ob = jnp.stack([ob[:, hh, :, hh] for hh in range(B_HEADS)], axis=1)
    ob = ob.transpose(0, 2, 1, 3).reshape(ts, 256)
    oc = oc.reshape(ns, C_HEADS, n_new, C_KV_RANK).transpose(0, 2, 1, 3).reshape(ts, 512)
    return oa, ob, oc


def _rope_tables(pos):
    half = C_ROPE_DIM // 2
    inv = ROPE_BASE ** (-jnp.arange(half, dtype=F32) / half)
    ang = pos.astype(F32)[:, None] * inv[None, :]
    cos, sin = jnp.cos(ang), jnp.sin(ang)
    qcs = jnp.tile(cos, (1, 8))
    qsn = jnp.concatenate([jnp.tile(-sin, (1, 4)), jnp.tile(sin, (1, 4))], axis=1)
    pad = jnp.zeros((pos.shape[0], 128 - C_ROPE_DIM), F32)
    kcs = jnp.concatenate([cos, cos, pad], axis=1)
    ksn = jnp.concatenate([-sin, sin, pad], axis=1)
    return qcs, qsn, kcs, ksn


def _uq_perm():
    per = C_NOPE_DIM + C_ROPE_DIM
    half = C_ROPE_DIM // 2
    nope = [h * per + d for h in range(C_HEADS) for d in range(C_NOPE_DIM)]
    r1 = [h * per + C_NOPE_DIM + j for h in range(C_HEADS) for j in range(half)]
    r2 = [h * per + C_NOPE_DIM + half + j for h in range(C_HEADS) for j in range(half)]
    return np.array(nope + r1 + r2, dtype=np.int32)


def _rope_placement():
    half = C_ROPE_DIM // 2
    pm = np.zeros((128, C_HEADS * 256), dtype=np.float32)
    for h in range(C_HEADS):
        for j in range(half):
            pm[h * half + j, h * 256 + C_KV_RANK + j] = 1.0
            pm[64 + h * half + j, h * 256 + C_KV_RANK + half + j] = 1.0
    return pm


def _layer_weights(l, norm_gains, diff_subln, mla_q_norm, mla_w_uq, mla_kv_norm, mla_w_ukv):
    eye = jnp.eye(C_HEADS, dtype=F32)
    w_ukv = mla_w_ukv[l].reshape(C_KV_RANK, C_HEADS, C_NOPE_DIM + C_V_DIM)
    w_uk = w_ukv[..., :C_NOPE_DIM]
    w_uv = w_ukv[..., C_NOPE_DIM:]
    uk_bd = jnp.einsum('chd,hg->hdgc', w_uk, eye)
    uk_bd = jnp.pad(uk_bd, ((0, 0), (0, 0), (0, 0), (0, 256 - C_KV_RANK))).reshape(256, C_HEADS * 256)
    wqc = jnp.concatenate([uk_bd, jnp.asarray(_rope_placement())], axis=0).astype(BF16)
    wuv = jnp.einsum('chv,hg->hcgv', w_uv, eye).reshape(C_HEADS * C_KV_RANK, C_HEADS * C_V_DIM).astype(BF16)
    g = norm_gains[l]
    return dict(
        gains=[g[i][None, :] for i in range(6)],
        subln=diff_subln[l][None, :],
        qn=mla_q_norm[l][None, :], kvn=mla_kv_norm[l][None, :],
        wuq=mla_w_uq[l][:, _uq_perm()].astype(BF16), wqc=wqc, wuv=wuv)


def kernel(x_prompt, x_sample, cache_a_k, cache_a_v, cache_b_k, cache_b_v, cache_c_latent, cache_c_krope,
           page_table, norm_gains, ffn_w_gate_up, ffn_w_down, w_in, w_out, diff_lambda, diff_subln,
           mla_q_norm, mla_w_uq, mla_kv_norm, mla_w_ukv):
    nb, seq, _ = x_prompt.shape
    ns, n_new, _ = x_sample.shape
    depth = w_in.shape[0]
    n_pool, page = cache_a_k.shape[1], cache_a_k.shape[2]
    past_len = page_table.shape[1] * page
    tp, ts = nb * seq, ns * n_new
    assert seq % TQ == 0 and TQ == MOBA_BLOCK and page == PAGE and past_len % MOBA_BLOCK == 0
    assert n_new == N_NEW and depth == 2

    tabs_p = _rope_tables(jnp.arange(seq, dtype=jnp.int32))
    tabs_s = _rope_tables(jnp.tile(past_len + jnp.arange(n_new, dtype=jnp.int32), ns))
    ca_k = cache_a_k.reshape(depth, n_pool, 2 * page, 128)
    ca_v = cache_a_v.reshape(depth, n_pool, 2 * page, 128)
    cb_kt = cache_b_k.transpose(0, 1, 3, 4, 2).reshape(depth, n_pool, 256, page)
    cb_vt = cache_b_v.transpose(0, 1, 3, 4, 2).reshape(depth, n_pool, 256, page)
    cc_krt = cache_c_krope.transpose(0, 1, 3, 2)

    wgu = ffn_w_gate_up.astype(BF16)
    wdn = ffn_w_down.astype(BF16)
    win = jnp.pad(w_in, ((0, 0), (0, 0), (0, IN_PAD - IN_COLS))).astype(BF16)
    wo = w_out.astype(BF16)

    h_p = x_prompt.reshape(tp, D_MODEL)
    h_s = x_sample.reshape(ts, D_MODEL)
    rows_p, rows_s = None, None
    for l in range(depth):
        lam_init = 0.8 - 0.6 * math.exp(-0.3 * l)
        w = _layer_weights(l, norm_gains, diff_subln, mla_q_norm, mla_w_uq, mla_kv_norm, mla_w_ukv)
        gn = w['gains']
        lp = diff_lambda[l]

        h1 = _ffn(h_p, gn[0], gn[1], wgu, wdn, (l, 0), 512)
        (aq, bq, bqf, qc, akb, avb, bkb, bvb, ckvb), rows_p = _proj(
            h1, gn[2], win, l, w['qn'], w['wuq'], w['wqc'], w['kvn'], tabs_p, 512, seq // 512, rows_p)
        oa = _attn_a(aq, akb, avb, lp, w['subln'], lam_init, nb, seq)
        ob = _attn_b(bq, bqf, bkb, rows_p[2], bvb, nb, seq)
        oc = _attn_c(qc, ckvb, nb, seq)
        h2 = _merge(h1, oa, ob, oc, w['wuv'], wo, l, gn[3], 512)
        h_p = _ffn(h2, gn[4], gn[5], wgu, wdn, (l, 1), 512)

        g1 = _ffn(h_s, gn[0], gn[1], wgu, wdn, (l, 0), ts)
        (aq, bq, bqf, qc, akb, avb, bkb, bvb, ckvb), rows_s = _proj(
            g1, gn[2], win, l, w['qn'], w['wuq'], w['wqc'], w['kvn'], tabs_s, ts, 1, rows_s)
        oa, ob, oc = _decode_attention(page_table, l, lam_init, lp, w['subln'], aq, akb, avb, bq, bkb, bvb,
                                       qc, ckvb, ca_k, ca_v, cb_kt, cb_vt, cache_c_latent, cc_krt)
        g2 = _merge(g1, oa, ob, oc, w['wuv'], wo, l, gn[3], ts)
        h_s = _ffn(g2, gn[4], gn[5], wgu, wdn, (l, 1), ts)

    def row_outputs(rows, n, t):
        ak, av, bk, bv, clat, ckr = rows
        return (ak.reshape(depth, n, t, A_KV_HEADS, 2 * A_QK_DIM), av.reshape(depth, n, t, A_KV_HEADS, A_V_DIM),
                bk.reshape(depth, n, t, B_HEADS, B_HEAD_DIM), bv.reshape(depth, n, t, B_HEADS, B_HEAD_DIM),
                clat.reshape(depth, n, t, C_KV_RANK), ckr.reshape(depth, n, t, C_ROPE_DIM))

    return (h_p.reshape(nb, seq, D_MODEL), h_s.reshape(ns, n_new, D_MODEL),
            *row_outputs(rows_p, nb, seq), *row_outputs(rows_s, ns, n_new))
```
